```python
import math
import jax
import jax.numpy as jnp
from jax import lax
import numpy as np

D_MODEL = 1024
BATCH = 8
SEQ = 4096
DEPTH = 1

GRID_W = 64
CTX_LEN = 256
F_GROUPS = 4
F_GROUP_DIM = 128
F_WIDTH = F_GROUPS * F_GROUP_DIM
DN_HEADS = 8
DN_HEAD_DIM = 128
DN_WIDTH = DN_HEADS * DN_HEAD_DIM
CONV_K = 3
CHUNK = 64
N_DIR = 2
IN_SPLITS = (F_WIDTH, F_WIDTH, 3 * DN_WIDTH, DN_WIDTH, N_DIR * DN_HEADS, N_DIR * DN_HEADS, D_MODEL, D_MODEL)
IN_COLS = sum(IN_SPLITS)
IN_OFFSETS = tuple(int(o) for o in np.cumsum(IN_SPLITS)[:-1])
DEEPNORM_ALPHA = (2 * DEPTH) ** 0.25
DEEPNORM_BETA = (8 * DEPTH) ** -0.25
EPS = 1e-6

kernel_name = 'hybrid_fourier_deltanet_dit'


def layer_norm(x):
    xf = x.astype(jnp.float32)
    mu = jnp.mean(xf, axis=-1, keepdims=True)
    var = jnp.mean(jnp.square(xf - mu), axis=-1, keepdims=True)
    return ((xf - mu) * lax.rsqrt(var + EPS)).astype(x.dtype)


def l2_normalize(x):
    xf = x.astype(jnp.float32)
    return xf * lax.rsqrt(jnp.sum(xf * xf, axis=-1, keepdims=True) + EPS)


def sincos_2d(rows, cols, dim, dtype):
    quarter = dim // 4
    omega = 1.0 / (10000.0 ** (jnp.arange(quarter, dtype=jnp.float32) / quarter))
    pr = jnp.arange(rows, dtype=jnp.float32)[:, None] * omega
    pc = jnp.arange(cols, dtype=jnp.float32)[:, None] * omega
    er = jnp.concatenate([jnp.sin(pr), jnp.cos(pr)], axis=-1)
    ec = jnp.concatenate([jnp.sin(pc), jnp.cos(pc)], axis=-1)
    pe = jnp.concatenate([jnp.broadcast_to(er[:, None, :], (rows, cols, dim // 2)),
                          jnp.broadcast_to(ec[None, :, :], (rows, cols, dim // 2))], axis=-1)
    return pe.reshape(rows * cols, dim).astype(dtype)


def centred_depthwise_conv(x, w):
    n = x.shape[1]
    pad = w.shape[0] // 2
    xp = jnp.pad(x, ((0, 0), (pad, pad), (0, 0)))
    out = xp[:, 0:n] * w[0]
    for j in range(1, w.shape[0]):
        out = out + xp[:, j:j + n] * w[j]
    return out


def gated_delta_chunked(q, k, v, beta, g, s0):
    f32 = jnp.float32
    bsz, n, h, dk = q.shape
    dv = v.shape[-1]
    nc = n // CHUNK

    def to_chunks(t):
        t = t.astype(f32).reshape((bsz, nc, CHUNK) + t.shape[2:])
        return jnp.moveaxis(t, 3, 1)

    q = to_chunks(q) * (dk ** -0.5)
    k = to_chunks(k)
    v = to_chunks(v)
    beta = to_chunks(beta)
    gc = jnp.cumsum(to_chunks(g), axis=-1)
    idx = jnp.arange(CHUNK)
    tril = idx[:, None] >= idx[None, :]
    strict = idx[:, None] > idx[None, :]
    decay = jnp.exp(jnp.where(tril, gc[..., :, None] - gc[..., None, :], -jnp.inf))
    kb = k * beta[..., None]
    a_mat = jnp.where(strict, jnp.einsum('bhnid,bhnjd->bhnij', kb, k) * decay, 0.0) + jnp.eye(CHUNK, dtype=f32)
    rhs = jnp.concatenate([v * beta[..., None], kb * jnp.exp(gc)[..., None]], axis=-1)
    sol = lax.linalg.triangular_solve(a_mat, rhs, left_side=True, lower=True, unit_diagonal=True)
    u, w = sol[..., :dv], sol[..., dv:]
    qk = jnp.where(tril, jnp.einsum('bhnid,bhnjd->bhnij', q, k) * decay, 0.0)

    def step(s, xs):
        q_c, k_c, u_c, w_c, qk_c, gc_c = xs
        v_new = u_c - jnp.einsum('bhcd,bhde->bhce', w_c, s)
        o_c = (jnp.einsum('bhcd,bhde->bhce', q_c * jnp.exp(gc_c)[..., None], s)
               + jnp.einsum('bhij,bhje->bhie', qk_c, v_new))
        g_last = gc_c[..., -1]
        k_dec = k_c * jnp.exp(g_last[..., None] - gc_c)[..., None]
        s = s * jnp.exp(g_last)[..., None, None] + jnp.einsum('bhcd,bhce->bhde', k_dec, v_new)
        return s, o_c

    xs = tuple(jnp.moveaxis(t, 2, 0) for t in (q, k, u, w, qk, gc))
    s_final, o = lax.scan(step, s0.astype(f32), xs)
    o = jnp.transpose(o, (1, 0, 3, 2, 4)).reshape(bsz, n, h, dv)
    return o, s_final


def deltanet_branch(qkv, dn_gate, beta_raw, decay_raw, conv_w, a_log, dt_bias, norm_w, w_dn_out, s0_f, s0_b):
    bsz, n, _ = qkv.shape
    qkv = jax.nn.silu(centred_depthwise_conv(qkv, conv_w))
    q, k, v = jnp.split(qkv, 3, axis=-1)
    q = l2_normalize(q.reshape(bsz, n, DN_HEADS, DN_HEAD_DIM))
    k = l2_normalize(k.reshape(bsz, n, DN_HEADS, DN_HEAD_DIM))
    v = v.reshape(bsz, n, DN_HEADS, DN_HEAD_DIM)
    beta = jax.nn.sigmoid(beta_raw.astype(jnp.float32)).reshape(bsz, n, N_DIR, DN_HEADS)
    g = -jnp.exp(a_log.astype(jnp.float32)) * jax.nn.softplus(
        decay_raw.astype(jnp.float32).reshape(bsz, n, N_DIR, DN_HEADS) + dt_bias.astype(jnp.float32))
    o_f, s_f = gated_delta_chunked(q, k, v, beta[:, :, 0], g[:, :, 0], s0_f)
    flip = lambda t: jnp.flip(t, axis=1)
    o_b, s_b = gated_delta_chunked(flip(q), flip(k), flip(v), flip(beta[:, :, 1]), flip(g[:, :, 1]), s0_b)
    o = o_f + flip(o_b)
    o = o * lax.rsqrt(jnp.mean(o * o, axis=-1, keepdims=True) + EPS) * norm_w.astype(jnp.float32)
    o = o.astype(qkv.dtype).reshape(bsz, n, DN_WIDTH) * jax.nn.silu(dn_gate)
    return o @ w_dn_out, s_f, s_b


def fourier_branch(f_val, f_gate, w_fmix, w_f_out):
    bsz, n, _ = f_val.shape
    u = f_val.astype(jnp.float32).reshape(bsz, n, F_GROUPS, F_GROUP_DIM)
    mixed = jnp.fft.fft2(u, axes=(1, 3), norm='ortho').real.astype(f_val.dtype)
    mixed = jnp.einsum('bngc,gcd->bngd', mixed, w_fmix).reshape(bsz, n, F_WIDTH)
    return (mixed * jax.nn.silu(f_gate)) @ w_f_out


def merge_branches(r_f, r_d, y_f, y_d, w_out):
    return (jax.nn.sigmoid(r_f) * y_f + jax.nn.sigmoid(r_d) * y_d) @ w_out


def setup_inputs(seed: int = 0) -> dict:
    key = jax.random.key(seed)
    ks = jax.random.split(key, 17)
    f32 = jnp.float32

    def nrm(k, shape, scale):
        return jax.random.normal(k, shape, f32) * scale

    x = nrm(ks[0], (BATCH, SEQ, D_MODEL), 1.0)
    c = nrm(ks[1], (BATCH, D_MODEL), 1.0)
    ctx = nrm(ks[2], (BATCH, CTX_LEN, D_MODEL), 1.0)
    c_ctx = nrm(ks[3], (D_MODEL,), 1.0)
    w_mod = nrm(ks[4], (DEPTH, D_MODEL, 3 * D_MODEL), 0.5 * D_MODEL ** -0.5)
    b_mod = nrm(ks[5], (DEPTH, 3 * D_MODEL), 0.02)
    w_in = nrm(ks[6], (DEPTH, D_MODEL, IN_COLS), D_MODEL ** -0.5)
    conv_w = nrm(ks[7], (DEPTH, CONV_K, 3 * DN_WIDTH), CONV_K ** -0.5)
    a_log = jnp.log(jax.random.uniform(ks[8], (DEPTH, N_DIR, DN_HEADS), f32, 1.0, 16.0))
    dt = jnp.exp(jax.random.uniform(ks[9], (DEPTH, N_DIR, DN_HEADS), f32, math.log(1e-3), math.log(1e-1)))
    dt_bias = dt + jnp.log(-jnp.expm1(-dt))
    dn_norm_w = 1.0 + nrm(ks[10], (DEPTH, DN_HEAD_DIM), 0.02)
    w_dn_out = nrm(ks[11], (DEPTH, DN_WIDTH, D_MODEL), DEEPNORM_BETA * DN_WIDTH ** -0.5)
    w_fmix = nrm(ks[12], (DEPTH, F_GROUPS, F_GROUP_DIM, F_GROUP_DIM), F_GROUP_DIM ** -0.5)
    w_f_out = nrm(ks[13], (DEPTH, F_WIDTH, D_MODEL), DEEPNORM_BETA * F_WIDTH ** -0.5)
    w_out = nrm(ks[14], (DEPTH, D_MODEL, D_MODEL), DEEPNORM_BETA * D_MODEL ** -0.5)
    ln_g = 1.0 + nrm(ks[15], (DEPTH, D_MODEL), 0.02)
    ln_b = nrm(ks[16], (DEPTH, D_MODEL), 0.02)
    return {'x': x, 'c': c, 'ctx': ctx, 'c_ctx': c_ctx, 'w_mod': w_mod, 'b_mod': b_mod,
            'w_in': w_in, 'conv_w': conv_w, 'a_log': a_log, 'dt_bias': dt_bias,
            'dn_norm_w': dn_norm_w, 'w_dn_out': w_dn_out, 'w_fmix': w_fmix, 'w_f_out': w_f_out,
            'w_out': w_out, 'ln_g': ln_g, 'ln_b': ln_b}


def reference(x, c, ctx, c_ctx, w_mod, b_mod, w_in, conv_w, a_log, dt_bias, dn_norm_w,
              w_dn_out, w_fmix, w_f_out, w_out, ln_g, ln_b):
    n_lat = x.shape[1]
    rows = n_lat // GRID_W
    x = x + sincos_2d(rows, GRID_W, D_MODEL, x.dtype)[None]
    for l in range(DEPTH):
        mod_x = jax.nn.silu(c) @ w_mod[l] + b_mod[l]
        mod_c = jax.nn.silu(c_ctx) @ w_mod[l] + b_mod[l]
        shift_x, scale_x, gate_x = jnp.split(mod_x[:, None, :], 3, axis=-1)
        shift_c, scale_c, gate_c = jnp.split(mod_c, 3, axis=-1)
        h_x = layer_norm(x) * (1.0 + scale_x) + shift_x
        h_c = layer_norm(ctx) * (1.0 + scale_c) + shift_c
        p_x = jnp.split(h_x @ w_in[l], IN_OFFSETS, axis=-1)
        p_c = jnp.split(h_c @ w_in[l], IN_OFFSETS, axis=-1)
        zero_state = jnp.zeros((ctx.shape[0], DN_HEADS, DN_HEAD_DIM, DN_HEAD_DIM), jnp.float32)
        y_dn_c, s_f, s_b = deltanet_branch(p_c[2], p_c[3], p_c[4], p_c[5], conv_w[l], a_log[l], dt_bias[l],
                                           dn_norm_w[l], w_dn_out[l], zero_state, zero_state)
        y_dn_x, _, _ = deltanet_branch(p_x[2], p_x[3], p_x[4], p_x[5], conv_w[l], a_log[l], dt_bias[l],
                                       dn_norm_w[l], w_dn_out[l], s_f, s_b)
        y_f_x = fourier_branch(p_x[0], p_x[1], w_fmix[l], w_f_out[l])
        out_x = merge_branches(p_x[6], p_x[7], y_f_x, y_dn_x, w_out[l])
        x_next = layer_norm(DEEPNORM_ALPHA * x + gate_x * out_x) * ln_g[l] + ln_b[l]
        if l + 1 < DEPTH:
            y_f_c = fourier_branch(p_c[0], p_c[1], w_fmix[l], w_f_out[l])
            out_c = merge_branches(p_c[6], p_c[7], y_f_c, y_dn_c, w_out[l])
            ctx = layer_norm(DEEPNORM_ALPHA * ctx + gate_c * out_c) * ln_g[l] + ln_b[l]
        x = x_next
    return x
```

```python
import functools
import math

import numpy as np
import jax
import jax.numpy as jnp
from jax import lax
from jax.experimental import pallas as pl
from jax.experimental.pallas import tpu as pltpu

F32 = jnp.float32
BF16 = jnp.bfloat16

D_MODEL = 1024
GRID_W = 64
F_GROUPS = 4
F_GROUP_DIM = 128
F_WIDTH = F_GROUPS * F_GROUP_DIM
DN_HEADS = 8
DN_HEAD_DIM = 128
DN_WIDTH = DN_HEADS * DN_HEAD_DIM
N_DIR = 2
CHUNK = 64
EPS = 1e-6
LANES = 128
AUX_GROUP = 16
VMEM_LIMIT = 56 * 1024 * 1024


def _silu(x):
    return x * jax.nn.sigmoid(x)


def _softplus(x):
    return jnp.maximum(x, 0.0) + jnp.log1p(jnp.exp(-jnp.abs(x)))


def _dot(a, b):
    return jnp.dot(a.astype(BF16), b.astype(BF16), preferred_element_type=F32)


def _dot_nt(a, b):
    return lax.dot_general(a.astype(BF16), b.astype(BF16), (((1,), (1,)), ((), ())),
                           preferred_element_type=F32)


def _split3(x):
    hi = x.astype(BF16)
    r1 = x - hi.astype(F32)
    mid = r1.astype(BF16)
    lo = (r1 - mid.astype(F32)).astype(BF16)
    return hi, mid, lo


def _layer_norm(x):
    mu = jnp.mean(x, axis=-1, keepdims=True)
    xc = x - mu
    var = jnp.mean(xc * xc, axis=-1, keepdims=True)
    return xc * lax.rsqrt(var + EPS)


def _add_pos(x, er_ref, ec_ref, t, tm):
    rows = tm // GRID_W
    half = D_MODEL // 2
    er = er_ref[pl.ds(t * rows, rows), :]
    ec = ec_ref[...]
    x3 = x.reshape(rows, GRID_W, D_MODEL)
    pe = jnp.concatenate([jnp.broadcast_to(er[:, None, :], (rows, GRID_W, half)),
                          jnp.broadcast_to(ec[None, :, :], (rows, GRID_W, half))], axis=-1)
    return (x3 + pe).reshape(tm, D_MODEL)


def _mod_kernel(c_ref, w_ref, b_ref, o_ref):
    s = _silu(c_ref[...])
    o_ref[...] = jnp.dot(s, w_ref[...], preferred_element_type=F32,
                         precision=lax.Precision.HIGHEST) + b_ref[...]


def _modulation(cc, w_mod, b_mod):
    rows = cc.shape[0]
    nblk = 3
    return pl.pallas_call(
        _mod_kernel,
        grid=(nblk,),
        in_specs=[pl.BlockSpec((rows, D_MODEL), lambda j: (0, 0)),
                  pl.BlockSpec((D_MODEL, D_MODEL), lambda j: (0, j)),
                  pl.BlockSpec((1, D_MODEL), lambda j: (0, j))],
        out_specs=pl.BlockSpec((rows, D_MODEL), lambda j: (0, j)),
        out_shape=jax.ShapeDtypeStruct((rows, 3 * D_MODEL), F32),
        compiler_params=pltpu.CompilerParams(dimension_semantics=("arbitrary",),
                                             vmem_limit_bytes=VMEM_LIMIT),
        name="mod",
    )(cc, w_mod, b_mod)


def _aux_from_raw(raw_b, raw_d, alog_l, dtb_l, lblk, tm):
    beta = jax.nn.sigmoid(raw_b)
    g = -jnp.exp(alog_l) * _softplus(raw_d + dtb_l)
    hi, mid, lo = _split3(g)
    pre = (jnp.dot(lblk, hi, preferred_element_type=F32)
           + jnp.dot(lblk, mid, preferred_element_type=F32)
           + jnp.dot(lblk, lo, preferred_element_type=F32))
    nck = tm // CHUNK
    pre3 = pre.reshape(nck, CHUNK, LANES)
    tot = jnp.broadcast_to(pre3[:, CHUNK - 1:CHUNK, :], (nck, CHUNK, LANES)).reshape(tm, LANES)
    suf = tot - pre + g
    lane = lax.broadcasted_iota(jnp.int32, (tm, LANES), 1)
    backward = (lane % AUX_GROUP) >= DN_HEADS
    gc = jnp.where(backward, suf, pre)
    egc = jnp.exp(gc)
    grp = lane // AUX_GROUP
    aux = jnp.where(grp == 0, beta,
          jnp.where(grp == 1, gc,
          jnp.where(grp == 2, egc,
          jnp.where(grp == 3, beta * egc,
          jnp.where(grp == 4, jnp.exp(tot - gc),
          jnp.where(grp == 5, jnp.exp(tot), 0.0))))))
    return aux


def _gct_from_raw(raw_t, alog_c, dtb_c, lblk, ublk):
    g = -jnp.exp(alog_c) * _softplus(raw_t + dtb_c)
    hi, mid, lo = _split3(g)
    pre = (jnp.dot(hi, ublk, preferred_element_type=F32)
           + jnp.dot(mid, ublk, preferred_element_type=F32)
           + jnp.dot(lo, ublk, preferred_element_type=F32))
    suf = (jnp.dot(hi, lblk, preferred_element_type=F32)
           + jnp.dot(mid, lblk, preferred_element_type=F32)
           + jnp.dot(lo, lblk, preferred_element_type=F32))
    row = lax.broadcasted_iota(jnp.int32, g.shape, 0)
    return jnp.where(row >= DN_HEADS, suf, pre)


def _inproj_kernel(*refs, tm, latent):
    if latent:
        (x_ref, mod_ref, er_ref, ec_ref, wf_ref, wqkv_ref, wg_ref, wr_ref, wab_ref, wad_ref, wdt_ref,
         lvec_ref, cvec_ref, lblk_ref, ublk_ref,
         fv_ref, fg_ref, qkv_ref, sg_ref, rf_ref, rd_ref, aux_ref, gct_ref) = refs
    else:
        (x_ref, mod_ref, wqkv_ref, wab_ref, wad_ref, wdt_ref,
         lvec_ref, cvec_ref, lblk_ref, ublk_ref,
         qkv_ref, aux_ref, gct_ref) = refs
    t = pl.program_id(1)
    x = x_ref[0]
    if latent:
        x = _add_pos(x, er_ref, ec_ref, t, tm)
    mod = mod_ref[0]
    shift = mod[:, 0:D_MODEL]
    scale = mod[:, D_MODEL:2 * D_MODEL]
    h = (_layer_norm(x) * (1.0 + scale) + shift).astype(BF16)

    qkv_ref[0] = jnp.dot(h, wqkv_ref[...], preferred_element_type=F32).astype(BF16)
    if latent:
        pf = jnp.dot(h, wf_ref[...], preferred_element_type=F32)
        fv_ref[0] = pf[:, :F_WIDTH].astype(BF16)
        fg_ref[0] = _silu(pf[:, F_WIDTH:]).astype(BF16)
        sg_ref[0] = _silu(jnp.dot(h, wg_ref[...], preferred_element_type=F32)).astype(BF16)
        pr = jnp.dot(h, wr_ref[...], preferred_element_type=F32)
        rf_ref[0] = jax.nn.sigmoid(pr[:, :D_MODEL]).astype(BF16)
        rd_ref[0] = jax.nn.sigmoid(pr[:, D_MODEL:]).astype(BF16)

    lblk = lblk_ref[...]
    raw_b = jnp.dot(h, wab_ref[...], preferred_element_type=F32)
    raw_d = jnp.dot(h, wad_ref[...], preferred_element_type=F32)
    aux_ref[0] = _aux_from_raw(raw_b, raw_d, lvec_ref[0:1, :], lvec_ref[1:2, :], lblk, tm)
    raw_t = lax.dot_general(wdt_ref[...], h, (((1,), (1,)), ((), ())), preferred_element_type=F32)
    gct_ref[0] = _gct_from_raw(raw_t, cvec_ref[:, 0:1], cvec_ref[:, 1:2], lblk, ublk_ref[...])


def _const_spec(shape):
    nd = len(shape)
    return pl.BlockSpec(shape, lambda b, t: (0,) * nd)


def _inproj(xin, mod3, mod_row_fn, tables, weights, consts, *, tm, latent):
    bsz, n, _ = xin.shape
    nt = n // tm
    wf, wqkv, wg, wr, wab, wad, wdt = weights
    lvec, cvec, lblk, ublk = consts
    tok = lambda w: pl.BlockSpec((1, tm, w), lambda b, t: (b, t, 0))
    x_spec = pl.BlockSpec((1, tm, D_MODEL), lambda b, t: (b, t, 0))
    mod_spec = pl.BlockSpec((1, 1, 3 * D_MODEL), lambda b, t: (mod_row_fn(b), 0, 0))
    gct_spec = pl.BlockSpec((1, AUX_GROUP, tm), lambda b, t: (b, 0, t))
    sds = lambda w, dt: jax.ShapeDtypeStruct((bsz, n, w), dt)
    gct_sds = jax.ShapeDtypeStruct((bsz, AUX_GROUP, n), F32)
    tail_in = [lvec, cvec, lblk, ublk]
    tail_specs = [_const_spec(a.shape) for a in tail_in]
    if latent:
        er, ec = tables
        ins = [xin, mod3, er, ec, wf, wqkv, wg, wr, wab, wad, wdt] + tail_in
        in_specs = ([x_spec, mod_spec] + [_const_spec(a.shape) for a in ins[2:11]] + tail_specs)
        out_specs = [tok(F_WIDTH), tok(F_WIDTH), tok(3 * DN_WIDTH), tok(DN_WIDTH), tok(D_MODEL),
                     tok(D_MODEL), tok(LANES), gct_spec]
        out_shape = [sds(F_WIDTH, BF16), sds(F_WIDTH, BF16), sds(3 * DN_WIDTH, BF16), sds(DN_WIDTH, BF16),
                     sds(D_MODEL, BF16), sds(D_MODEL, BF16), sds(LANES, F32), gct_sds]
    else:
        ins = [xin, mod3, wqkv, wab, wad, wdt] + tail_in
        in_specs = ([x_spec, mod_spec] + [_const_spec(a.shape) for a in ins[2:6]] + tail_specs)
        out_specs = [tok(3 * DN_WIDTH), tok(LANES), gct_spec]
        out_shape = [sds(3 * DN_WIDTH, BF16), sds(LANES, F32), gct_sds]
    return pl.pallas_call(
        functools.partial(_inproj_kernel, tm=tm, latent=latent),
        grid=(bsz, nt),
        in_specs=in_specs,
        out_specs=out_specs,
        out_shape=out_shape,
        compiler_params=pltpu.CompilerParams(dimension_semantics=("arbitrary", "arbitrary"),
                                             vmem_limit_bytes=VMEM_LIMIT),
        name="inproj_latent" if latent else "inproj_ctx",
    )(*ins)


def _conv_silu(ref, w_ref, c, nc):
    r0 = pl.multiple_of(c * CHUNK, CHUNK)
    cur = ref[0, pl.ds(r0, CHUNK), :].astype(F32)
    p0 = pl.multiple_of(jnp.maximum(r0 - 16, 0), 16)
    n0 = pl.multiple_of(jnp.minimum(r0 + CHUNK, (nc - 1) * CHUNK + CHUNK - 16), 16)
    prev = ref[0, pl.ds(p0, 16), :].astype(F32)[15:16, :] * (c > 0).astype(F32)
    nxt = ref[0, pl.ds(n0, 16), :].astype(F32)[0:1, :] * (c < nc - 1).astype(F32)
    row = lax.broadcasted_iota(jnp.int32, (CHUNK, LANES), 0)
    xm1 = jnp.where(row == 0, prev, pltpu.roll(cur, 1, axis=0))
    xp1 = jnp.where(row == CHUNK - 1, nxt, pltpu.roll(cur, CHUNK - 1, axis=0))
    w = w_ref[...]
    return _silu(xm1 * w[0:1, :] + cur * w[1:2, :] + xp1 * w[2:3, :])


def _unit_triangular_inverse(l):
    ri = lax.broadcasted_iota(jnp.int32, (CHUNK, CHUNK), 0)
    ci = lax.broadcasted_iota(jnp.int32, (CHUNK, CHUNK), 1)
    same = lambda k: (ri >> k) == (ci >> k)
    x = jnp.where(ri == ci, 1.0, 0.0) - jnp.where(same(1), l, 0.0)
    for k in range(1, int(math.log2(CHUNK))):
        off = jnp.where(same(k + 1) & jnp.logical_not(same(k)), l, 0.0)
        x = x - _dot(_dot(x, off), x)
    return x


def _dn_prepare(q_ref, k_ref, v_ref, aux_ref, gt_ref, cw_refs, h, c, nc, pos_f, pos_b, scr):
    u_s, w_s, qg_s, kdt_s, qk_s, egt_s = scr
    r0 = pl.multiple_of(c * CHUNK, CHUNK)
    q = _conv_silu(q_ref, cw_refs[0], c, nc)
    k = _conv_silu(k_ref, cw_refs[1], c, nc)
    v = _conv_silu(v_ref, cw_refs[2], c, nc)
    q = q * (lax.rsqrt(jnp.sum(q * q, axis=-1, keepdims=True) + EPS) * (DN_HEAD_DIM ** -0.5))
    k = k * lax.rsqrt(jnp.sum(k * k, axis=-1, keepdims=True) + EPS)
    kk = _dot_nt(k, k)
    qk = _dot_nt(q, k)
    aux = pltpu.roll(aux_ref[0, pl.ds(r0, CHUNK), :], (LANES - h) % LANES, axis=1)
    ri = lax.broadcasted_iota(jnp.int32, (CHUNK, CHUNK), 0)
    ci = lax.broadcasted_iota(jnp.int32, (CHUNK, CHUNK), 1)
    qk_dirs = []
    for d, pos in ((0, pos_f), (1, pos_b)):
        o = d * DN_HEADS
        col = lambda j: aux[:, j * AUX_GROUP + o:j * AUX_GROUP + o + 1]
        beta, gc, egc, begc, edec, egt = (col(j) for j in range(6))
        gr = gt_ref[0, h + o, pl.ds(c, 1), :]
        incl = (ri >= ci) if d == 0 else (ri <= ci)
        strict = (ri > ci) if d == 0 else (ri < ci)
        decay = jnp.exp(jnp.where(incl, gc - gr, -jnp.inf))
        l = jnp.where(strict, kk * decay * beta, 0.0)
        ainv = _unit_triangular_inverse(l)
        rhs = jnp.concatenate([v * beta, k * begc], axis=-1)
        sol = _dot(ainv, rhs)
        p0 = pl.multiple_of(pos * CHUNK, CHUNK)
        u_s[d, pl.ds(p0, CHUNK), :] = sol[:, :DN_HEAD_DIM]
        w_s[d, pl.ds(p0, CHUNK), :] = sol[:, DN_HEAD_DIM:].astype(BF16)
        qg_s[d, pl.ds(p0, CHUNK), :] = (q * egc).astype(BF16)
        kdt_s[d, pos] = (k * edec).T.astype(BF16)
        egt_s[d, pos] = jnp.broadcast_to(egt[0:1, :], (8, LANES))
        qk_dirs.append((pos, qk * decay))
    for d, (pos, m) in enumerate(qk_dirs):
        p0 = pl.multiple_of(pos * CHUNK, CHUNK)
        qk_s[d, pl.ds(p0, CHUNK), :] = m.astype(BF16)


def _dn_step(d, pos, scr, s_ref):
    u_s, w_s, qg_s, kdt_s, qk_s, egt_s = scr
    p0 = pl.multiple_of(pos * CHUNK, CHUNK)
    s = s_ref[d]
    sb = s.astype(BF16)
    ws = jnp.dot(w_s[d, pl.ds(p0, CHUNK), :], sb, preferred_element_type=F32)
    qs = jnp.dot(qg_s[d, pl.ds(p0, CHUNK), :], sb, preferred_element_type=F32)
    v_new = (u_s[d, pl.ds(p0, CHUNK), :] - ws).astype(BF16)
    o = qs + jnp.dot(qk_s[d, pl.ds(p0, CHUNK), :], v_new, preferred_element_type=F32)
    egt = egt_s[d, pos]
    s3 = s.reshape(DN_HEAD_DIM // 8, 8, LANES) * egt[None]
    s_ref[d] = s3.reshape(DN_HEAD_DIM, LANES) + jnp.dot(kdt_s[d, pos], v_new, preferred_element_type=F32)
    return o


def _dn_kernel(qx, kx, vx, auxx, gtx, qc, kc, vc, auxc, gtc, cwq, cwk, cwv, sg_ref, nw_ref, out_ref,
               u_s, w_s, qg_s, kdt_s, qk_s, egt_s, of_s, ob_s, s_ref, *, ncx, ncc):
    h = pl.program_id(1)
    scr = (u_s, w_s, qg_s, kdt_s, qk_s, egt_s)
    cws = (cwq, cwk, cwv)
    nct = ncx + ncc

    def prep_ctx(c, carry):
        _dn_prepare(qc, kc, vc, auxc, gtc, cws, h, c, ncc, c, ncc - 1 - c, scr)
        return carry

    def prep_lat(c, carry):
        _dn_prepare(qx, kx, vx, auxx, gtx, cws, h, c, ncx, ncc + c, nct - 1 - c, scr)
        return carry

    lax.fori_loop(0, ncc, prep_ctx, 0)
    lax.fori_loop(0, ncx, prep_lat, 0)

    s_ref[...] = jnp.zeros_like(s_ref)

    def scan_ctx(i, carry):
        _dn_step(0, i, scr, s_ref)
        _dn_step(1, i, scr, s_ref)
        return carry

    def scan_lat(i, carry):
        pos = ncc + i
        o_f = _dn_step(0, pos, scr, s_ref)
        o_b = _dn_step(1, pos, scr, s_ref)
        of_s[pl.ds(pl.multiple_of(i * CHUNK, CHUNK), CHUNK), :] = o_f
        ob_s[pl.ds(pl.multiple_of((ncx - 1 - i) * CHUNK, CHUNK), CHUNK), :] = o_b
        return carry

    lax.fori_loop(0, ncc, scan_ctx, 0)
    lax.fori_loop(0, ncx, scan_lat, 0)

    nw = nw_ref[...]

    def finish(c, carry):
        r0 = pl.multiple_of(c * CHUNK, CHUNK)
        o = of_s[pl.ds(r0, CHUNK), :] + ob_s[pl.ds(r0, CHUNK), :]
        o = o * lax.rsqrt(jnp.mean(o * o, axis=-1, keepdims=True) + EPS) * nw
        out_ref[0, pl.ds(r0, CHUNK), :] = (o * sg_ref[0, pl.ds(r0, CHUNK), :].astype(F32)).astype(BF16)
        return carry

    lax.fori_loop(0, ncx, finish, 0)


def _deltanet(qkv_x, aux_x, gct_x, qkv_c, aux_c, gct_c, conv_w, sg, norm_w):
    bsz, n, _ = qkv_x.shape
    nctx = qkv_c.shape[1]
    ncx, ncc = n // CHUNK, nctx // CHUNK
    nct = ncx + ncc
    gct_x = gct_x.reshape(bsz, AUX_GROUP, ncx, CHUNK)
    gct_c = gct_c.reshape(bsz, AUX_GROUP, ncc, CHUNK)
    col = lambda rows, off: pl.BlockSpec((1, rows, DN_HEAD_DIM), lambda b, h: (b, 0, off + h))
    whole = lambda a: pl.BlockSpec((1,) + a.shape[1:], lambda b, h: (b,) + (0,) * (a.ndim - 1))
    cw = lambda off: pl.BlockSpec((3, DN_HEAD_DIM), lambda b, h: (0, off + h))
    in_specs = [col(n, 0), col(n, DN_HEADS), col(n, 2 * DN_HEADS), whole(aux_x), whole(gct_x),
                col(nctx, 0), col(nctx, DN_HEADS), col(nctx, 2 * DN_HEADS), whole(aux_c), whole(gct_c),
                cw(0), cw(DN_HEADS), cw(2 * DN_HEADS),
                col(n, 0), pl.BlockSpec((1, DN_HEAD_DIM), lambda b, h: (0, 0))]
    scratch = [pltpu.VMEM((N_DIR, nct * CHUNK, DN_HEAD_DIM), F32),
               pltpu.VMEM((N_DIR, nct * CHUNK, DN_HEAD_DIM), BF16),
               pltpu.VMEM((N_DIR, nct * CHUNK, DN_HEAD_DIM), BF16),
               pltpu.VMEM((N_DIR, nct, DN_HEAD_DIM, CHUNK), BF16),
               pltpu.VMEM((N_DIR, nct * CHUNK, CHUNK), BF16),
               pltpu.VMEM((N_DIR, nct, 8, LANES), F32),
               pltpu.VMEM((n, DN_HEAD_DIM), F32),
               pltpu.VMEM((n, DN_HEAD_DIM), F32),
               pltpu.VMEM((N_DIR, DN_HEAD_DIM, DN_HEAD_DIM), F32)]
    return pl.pallas_call(
        functools.partial(_dn_kernel, ncx=ncx, ncc=ncc),
        grid=(bsz, DN_HEADS),
        in_specs=in_specs,
        out_specs=col(n, 0),
        out_shape=jax.ShapeDtypeStruct((bsz, n, DN_WIDTH), BF16),
        scratch_shapes=scratch,
        compiler_params=pltpu.CompilerParams(dimension_semantics=("arbitrary", "arbitrary"),
                                             vmem_limit_bytes=VMEM_LIMIT),
        name="deltanet",
    )(qkv_x, qkv_x, qkv_x, aux_x, gct_x, qkv_c, qkv_c, qkv_c, aux_c, gct_c,
      conv_w, conv_w, conv_w, sg, norm_w)


_F_ROWS = 128


def _fourier_kernel(fv_ref, m_ref, r_ref, out_ref, pq_s, *, n4):
    k1 = pl.program_id(1)
    rb = min(_F_ROWS, n4)

    def combo(x, k):
        if k == 0:
            return x[0] + x[1] + x[2] + x[3], None
        if k == 2:
            return x[0] - x[1] + x[2] - x[3], None
        if k == 1:
            return x[0] - x[2], x[3] - x[1]
        return x[0] - x[2], x[1] - x[3]

    r = r_ref[...]
    for k in range(4):
        @pl.when(k1 == k)
        def _():
            def body(i, carry):
                r0 = pl.multiple_of(i * rb, rb)
                x = [fv_ref[0, pl.ds(j * n4 + r0, rb), :].astype(F32) for j in range(4)]
                d, e = combo(x, k)
                for g in range(F_GROUPS):
                    sl = slice(g * F_GROUP_DIM, (g + 1) * F_GROUP_DIM)
                    if e is None:
                        pq = jnp.dot(d[:, sl].astype(BF16), r[:F_GROUP_DIM, :], preferred_element_type=F32)
                    else:
                        de = jnp.concatenate([d[:, sl], e[:, sl]], axis=-1).astype(BF16)
                        pq = jnp.dot(de, r, preferred_element_type=F32)
                    pq_s[pl.ds(r0, rb), sl] = pq[:, :F_GROUP_DIM].astype(BF16)
                    pq_s[pl.ds(n4 + r0, rb), sl] = pq[:, F_GROUP_DIM:].astype(BF16)
                return carry
            lax.fori_loop(0, n4 // rb, body, 0)

    def rows(i, carry):
        r0 = pl.multiple_of(i * rb, rb)
        res = jnp.dot(m_ref[0, pl.ds(r0, rb), :], pq_s[...], preferred_element_type=F32)
        for g in range(F_GROUPS):
            out_ref[0, g, pl.ds(k1 + 4 * r0, rb, stride=4), :] = res[:, g * F_GROUP_DIM:(g + 1) * F_GROUP_DIM]
        return carry
    lax.fori_loop(0, n4 // rb, rows, 0)


def _fourier(fv, mcat, rmat):
    bsz, n, _ = fv.shape
    n4 = n // 4
    return pl.pallas_call(
        functools.partial(_fourier_kernel, n4=n4),
        grid=(bsz, 4),
        in_specs=[pl.BlockSpec((1, n, F_WIDTH), lambda b, k: (b, 0, 0)),
                  pl.BlockSpec((1, n4, 2 * n4), lambda b, k: (k, 0, 0)),
                  pl.BlockSpec((2 * F_GROUP_DIM, 2 * F_GROUP_DIM), lambda b, k: (0, 0))],
        out_specs=pl.BlockSpec((1, F_GROUPS, n, F_GROUP_DIM), lambda b, k: (b, 0, 0, 0)),
        out_shape=jax.ShapeDtypeStruct((bsz, F_GROUPS, n, F_GROUP_DIM), F32),
        scratch_shapes=[pltpu.VMEM((2 * n4, F_WIDTH), BF16)],
        compiler_params=pltpu.CompilerParams(dimension_semantics=("arbitrary", "arbitrary"),
                                             vmem_limit_bytes=VMEM_LIMIT),
        name="fourier",
    )(fv, mcat, rmat)


def _merge_kernel(x_ref, mod_ref, er_ref, ec_ref, mix_ref, fg_ref, og_ref, rf_ref, rd_ref,
                  wfm_ref, wfo_ref, wdo_ref, wo_ref, lng_ref, lnb_ref, out_ref, *, tm, alpha):
    t = pl.program_id(1)
    mixed = jnp.concatenate(
        [jnp.dot(mix_ref[0, g].astype(BF16), wfm_ref[g], preferred_element_type=F32)
         for g in range(F_GROUPS)], axis=-1)
    y_f = jnp.dot((mixed * fg_ref[0].astype(F32)).astype(BF16), wfo_ref[...], preferred_element_type=F32)
    y_d = jnp.dot(og_ref[0], wdo_ref[...], preferred_element_type=F32)
    m = rf_ref[0].astype(F32) * y_f + rd_ref[0].astype(F32) * y_d
    o = jnp.dot(m.astype(BF16), wo_ref[...], preferred_element_type=F32)
    gate = mod_ref[0][:, 2 * D_MODEL:3 * D_MODEL]
    x = _add_pos(x_ref[0], er_ref, ec_ref, t, tm)
    out_ref[0] = _layer_norm(alpha * x + gate * o) * lng_ref[...] + lnb_ref[...]


def _merge(x, mod3, er, ec, mixed, fg, og, rf, rd, wfm, wfo, wdo, wo, ln_g, ln_b, *, tm, alpha):
    bsz, n, _ = x.shape
    tok = lambda w: pl.BlockSpec((1, tm, w), lambda b, t: (b, t, 0))
    consts = [wfm, wfo, wdo, wo, ln_g, ln_b]
    in_specs = [tok(D_MODEL),
                pl.BlockSpec((1, 1, 3 * D_MODEL), lambda b, t: (b, 0, 0)),
                _const_spec(er.shape), _const_spec(ec.shape),
                pl.BlockSpec((1, F_GROUPS, tm, F_GROUP_DIM), lambda b, t: (b, 0, t, 0)),
                tok(F_WIDTH), tok(DN_WIDTH), tok(D_MODEL), tok(D_MODEL)]
    in_specs += [_const_spec(a.shape) for a in consts]
    return pl.pallas_call(
        functools.partial(_merge_kernel, tm=tm, alpha=alpha),
        grid=(bsz, n // tm),
        in_specs=in_specs,
        out_specs=tok(D_MODEL),
        out_shape=jax.ShapeDtypeStruct((bsz, n, D_MODEL), F32),
        compiler_params=pltpu.CompilerParams(dimension_semantics=("arbitrary", "arbitrary"),
                                             vmem_limit_bytes=VMEM_LIMIT),
        name="merge",
    )(x, mod3, er, ec, mixed, fg, og, rf, rd, *consts)


def _pos_tables(rows):
    quarter = D_MODEL // 4
    omega = 1.0 / (10000.0 ** (np.arange(quarter, dtype=np.float64) / quarter))
    pr = np.arange(rows, dtype=np.float64)[:, None] * omega
    pc = np.arange(GRID_W, dtype=np.float64)[:, None] * omega
    er = np.concatenate([np.sin(pr), np.cos(pr)], axis=-1)
    ec = np.concatenate([np.sin(pc), np.cos(pc)], axis=-1)
    return jnp.asarray(er, F32), jnp.asarray(ec, F32)


def _chunk_sum_matrices(tm):
    i = np.arange(tm)
    same = (i[:, None] // CHUNK) == (i[None, :] // CHUNK)
    lower = same & (i[:, None] >= i[None, :])
    upper = same & (i[:, None] <= i[None, :])
    return jnp.asarray(lower, BF16), jnp.asarray(upper, BF16)


def _channel_dft_matrix():
    c = np.arange(F_GROUP_DIM)
    ang = 2.0 * np.pi * ((c[:, None] * c[None, :]) % F_GROUP_DIM) / F_GROUP_DIM
    cc, sc = np.cos(ang), np.sin(ang)
    r = np.block([[cc, sc], [sc, -cc]]) / math.sqrt(F_GROUP_DIM)
    return jnp.asarray(r, F32).astype(BF16)


def _position_dft_matrices(n):
    n4 = n // 4
    k1 = jnp.arange(4, dtype=jnp.int32)[:, None, None]
    k2 = jnp.arange(n4, dtype=jnp.int32)[None, :, None]
    n2 = jnp.arange(n4, dtype=jnp.int32)[None, None, :]
    ang = ((n2 * (k1 + 4 * k2)) % n).astype(F32) * (2.0 * math.pi / n)
    scale = 1.0 / math.sqrt(n)
    return jnp.concatenate([jnp.cos(ang) * scale, -jnp.sin(ang) * scale], axis=-1).astype(BF16)


def _aux_weights(w_beta, w_decay, a_log, dt_bias):
    reps = LANES // AUX_GROUP
    wab = jnp.tile(w_beta, (1, reps)).astype(BF16)
    wad = jnp.tile(w_decay, (1, reps)).astype(BF16)
    wdt = w_decay.T.astype(BF16)
    al = a_log.reshape(1, AUX_GROUP).astype(F32)
    db = dt_bias.reshape(1, AUX_GROUP).astype(F32)
    lvec = jnp.concatenate([jnp.tile(al, (1, reps)), jnp.tile(db, (1, reps)),
                            jnp.zeros((6, LANES), F32)], axis=0)
    cvec = jnp.concatenate([al.T, db.T], axis=1)
    return wab, wad, wdt, lvec, cvec


def kernel(x, c, ctx, c_ctx, w_mod, b_mod, w_in, conv_w, a_log, dt_bias, dn_norm_w, w_dn_out, w_fmix,
           w_f_out, w_out, ln_g, ln_b):
    depth = w_mod.shape[0]
    assert depth == 1, "single-layer configuration"
    bsz, n, _ = x.shape
    nctx = ctx.shape[1]
    assert n % (4 * CHUNK) == 0 and nctx % CHUNK == 0 and bsz <= 8
    alpha = (2 * depth) ** 0.25
    tm = min(512, n)
    tmc = min(256, nctx)

    cc = jnp.zeros((16, D_MODEL), F32).at[:bsz].set(c).at[8].set(c_ctx)
    mod3 = _modulation(cc, w_mod[0], b_mod[0].reshape(1, -1)).reshape(16, 1, 3 * D_MODEL)

    w = w_in[0]
    o0, o1, o2, o3, o4, o5, o6 = (int(v) for v in np.cumsum(
        (F_WIDTH, F_WIDTH, 3 * DN_WIDTH, DN_WIDTH, AUX_GROUP, AUX_GROUP, D_MODEL)))
    wf = w[:, :o1].astype(BF16)
    wqkv = w[:, o1:o2].astype(BF16)
    wg = w[:, o2:o3].astype(BF16)
    wr = w[:, o5:].astype(BF16)
    wab, wad, wdt, lvec, cvec = _aux_weights(w[:, o3:o4], w[:, o4:o5], a_log[0], dt_bias[0])
    er, ec = _pos_tables(n // GRID_W)

    lblk, ublk = _chunk_sum_matrices(tm)
    fv, fg, qkv_x, sg, rf, rd, aux_x, gct_x = _inproj(
        x, mod3, lambda b: b, (er, ec), (wf, wqkv, wg, wr, wab, wad, wdt), (lvec, cvec, lblk, ublk),
        tm=tm, latent=True)
    lblk_c, ublk_c = _chunk_sum_matrices(tmc)
    qkv_c, aux_c, gct_c = _inproj(
        ctx, mod3, lambda b: 8, None, (wf, wqkv, wg, wr, wab, wad, wdt), (lvec, cvec, lblk_c, ublk_c),
        tm=tmc, latent=False)

    og = _deltanet(qkv_x, aux_x, gct_x, qkv_c, aux_c, gct_c, conv_w[0], sg,
                   dn_norm_w[0].reshape(1, -1).astype(F32))
    mixed = _fourier(fv, _position_dft_matrices(n), _channel_dft_matrix())
    return _merge(x, mod3, er, ec, mixed, fg, og, rf, rd,
                  w_fmix[0].astype(BF16), w_f_out[0].astype(BF16), w_dn_out[0].astype(BF16),
                  w_out[0].astype(BF16), ln_g[0].reshape(1, -1), ln_b[0].reshape(1, -1),
                  tm=tm, alpha=alpha)
```

```python
import functools
import math

import numpy as np
import jax
import jax.numpy as jnp
from jax import lax
from jax.experimental import pallas as pl
from jax.experimental.pallas import tpu as pltpu

F32 = jnp.float32
BF16 = jnp.bfloat16

D_MODEL = 1024
GRID_W = 64
F_GROUPS = 4
F_GROUP_DIM = 128
F_WIDTH = F_GROUPS * F_GROUP_DIM
DN_HEADS = 8
DN_HEAD_DIM = 128
DN_WIDTH = DN_HEADS * DN_HEAD_DIM
N_DIR = 2
CHUNK = 64
EPS = 1e-6
LANES = 128
AUX_GROUP = 16
VMEM_LIMIT = 56 * 1024 * 1024


def _silu(x):
    return x * jax.nn.sigmoid(x)


def _softplus(x):
    return jnp.maximum(x, 0.0) + jnp.log1p(jnp.exp(-jnp.abs(x)))


def _dot(a, b):
    return jnp.dot(a.astype(BF16), b.astype(BF16), preferred_element_type=F32)


def _dot_nt(a, b):
    return lax.dot_general(a.astype(BF16), b.astype(BF16), (((1,), (1,)), ((), ())),
                           preferred_element_type=F32)


def _split3(x):
    hi = x.astype(BF16)
    r1 = x - hi.astype(F32)
    mid = r1.astype(BF16)
    lo = (r1 - mid.astype(F32)).astype(BF16)
    return hi, mid, lo


def _layer_norm(x):
    mu = jnp.mean(x, axis=-1, keepdims=True)
    xc = x - mu
    var = jnp.mean(xc * xc, axis=-1, keepdims=True)
    return xc * lax.rsqrt(var + EPS)


def _add_pos(x, er_ref, ec_ref, t, tm):
    rows = tm // GRID_W
    half = D_MODEL // 2
    er = er_ref[pl.ds(t * rows, rows), :]
    ec = ec_ref[...]
    x3 = x.reshape(rows, GRID_W, D_MODEL)
    pe = jnp.concatenate([jnp.broadcast_to(er[:, None, :], (rows, GRID_W, half)),
                          jnp.broadcast_to(ec[None, :, :], (rows, GRID_W, half))], axis=-1)
    return (x3 + pe).reshape(tm, D_MODEL)


def _mod_kernel(c_ref, w_ref, b_ref, o_ref):
    s = _silu(c_ref[...])
    o_ref[...] = jnp.dot(s, w_ref[...], preferred_element_type=F32,
                         precision=lax.Precision.HIGHEST) + b_ref[...]


def _modulation(cc, w_mod, b_mod):
    rows = cc.shape[0]
    nblk = 3
    return pl.pallas_call(
        _mod_kernel,
        grid=(nblk,),
        in_specs=[pl.BlockSpec((rows, D_MODEL), lambda j: (0, 0)),
                  pl.BlockSpec((D_MODEL, D_MODEL), lambda j: (0, j)),
                  pl.BlockSpec((1, D_MODEL), lambda j: (0, j))],
        out_specs=pl.BlockSpec((rows, D_MODEL), lambda j: (0, j)),
        out_shape=jax.ShapeDtypeStruct((rows, 3 * D_MODEL), F32),
        compiler_params=pltpu.CompilerParams(dimension_semantics=("arbitrary",),
                                             vmem_limit_bytes=VMEM_LIMIT),
        name="mod",
    )(cc, w_mod, b_mod)


def _aux_from_raw(raw_b, raw_d, alog_l, dtb_l, lblk, tm):
    beta = jax.nn.sigmoid(raw_b)
    g = -jnp.exp(alog_l) * _softplus(raw_d + dtb_l)
    hi, mid, lo = _split3(g)
    pre = (jnp.dot(lblk, hi, preferred_element_type=F32)
           + jnp.dot(lblk, mid, preferred_element_type=F32)
           + jnp.dot(lblk, lo, preferred_element_type=F32))
    nck = tm // CHUNK
    pre3 = pre.reshape(nck, CHUNK, LANES)
    tot = jnp.broadcast_to(pre3[:, CHUNK - 1:CHUNK, :], (nck, CHUNK, LANES)).reshape(tm, LANES)
    suf = tot - pre + g
    lane = lax.broadcasted_iota(jnp.int32, (tm, LANES), 1)
    backward = (lane % AUX_GROUP) >= DN_HEADS
    gc = jnp.where(backward, suf, pre)
    egc = jnp.exp(gc)
    grp = lane // AUX_GROUP
    aux = jnp.where(grp == 0, beta,
          jnp.where(grp == 1, gc,
          jnp.where(grp == 2, egc,
          jnp.where(grp == 3, beta * egc,
          jnp.where(grp == 4, jnp.exp(tot - gc),
          jnp.where(grp == 5, jnp.exp(tot), 0.0))))))
    return aux


def _gct_from_raw(raw_t, alog_c, dtb_c, lblk, ublk):
    g = -jnp.exp(alog_c) * _softplus(raw_t + dtb_c)
    hi, mid, lo = _split3(g)
    pre = (jnp.dot(hi, ublk, preferred_element_type=F32)
           + jnp.dot(mid, ublk, preferred_element_type=F32)
           + jnp.dot(lo, ublk, preferred_element_type=F32))
    suf = (jnp.dot(hi, lblk, preferred_element_type=F32)
           + jnp.dot(mid, lblk, preferred_element_type=F32)
           + jnp.dot(lo, lblk, preferred_element_type=F32))
    row = lax.broadcasted_iota(jnp.int32, g.shape, 0)
    return jnp.where(row >= DN_HEADS, suf, pre)


def _inproj_kernel(*refs, tm, latent):
    if latent:
        (x_ref, mod_ref, er_ref, ec_ref, wf_ref, wqkv_ref, wg_ref, wr_ref, wab_ref, wad_ref, wdt_ref,
         lvec_ref, cvec_ref, lblk_ref, ublk_ref,
         fv_ref, fg_ref, qkv_ref, sg_ref, rf_ref, rd_ref, aux_ref, gct_ref) = refs
    else:
        (x_ref, mod_ref, wqkv_ref, wab_ref, wad_ref, wdt_ref,
         lvec_ref, cvec_ref, lblk_ref, ublk_ref,
         qkv_ref, aux_ref, gct_ref) = refs
    t = pl.program_id(1)
    x = x_ref[0]
    if latent:
        x = _add_pos(x, er_ref, ec_ref, t, tm)
    mod = mod_ref[0]
    shift = mod[:, 0:D_MODEL]
    scale = mod[:, D_MODEL:2 * D_MODEL]
    h = (_layer_norm(x) * (1.0 + scale) + shift).astype(BF16)

    qkv_ref[0] = jnp.dot(h, wqkv_ref[...], preferred_element_type=F32).astype(BF16)
    if latent:
        pf = jnp.dot(h, wf_ref[...], preferred_element_type=F32)
        fv_ref[0] = pf[:, :F_WIDTH].astype(BF16)
        fg_ref[0] = _silu(pf[:, F_WIDTH:]).astype(BF16)
        sg_ref[0] = _silu(jnp.dot(h, wg_ref[...], preferred_element_type=F32)).astype(BF16)
        pr = jnp.dot(h, wr_ref[...], preferred_element_type=F32)
        rf_ref[0] = jax.nn.sigmoid(pr[:, :D_MODEL]).astype(BF16)
        rd_ref[0] = jax.nn.sigmoid(pr[:, D_MODEL:]).astype(BF16)

    lblk = lblk_ref[...]
    raw_b = jnp.dot(h, wab_ref[...], preferred_element_type=F32)
    raw_d = jnp.dot(h, wad_ref[...], preferred_element_type=F32)
    aux_ref[0] = _aux_from_raw(raw_b, raw_d, lvec_ref[0:1, :], lvec_ref[1:2, :], lblk, tm)
    raw_t = lax.dot_general(wdt_ref[...], h, (((1,), (1,)), ((), ())), preferred_element_type=F32)
    gct_ref[0] = _gct_from_raw(raw_t, cvec_ref[:, 0:1], cvec_ref[:, 1:2], lblk, ublk_ref[...])


def _const_spec(shape):
    nd = len(shape)
    return pl.BlockSpec(shape, lambda b, t: (0,) * nd)


def _inproj(xin, mod3, mod_row_fn, tables, weights, consts, *, tm, latent):
    bsz, n, _ = xin.shape
    nt = n // tm
    wf, wqkv, wg, wr, wab, wad, wdt = weights
    lvec, cvec, lblk, ublk = consts
    tok = lambda w: pl.BlockSpec((1, tm, w), lambda b, t: (b, t, 0))
    x_spec = pl.BlockSpec((1, tm, D_MODEL), lambda b, t: (b, t, 0))
    mod_spec = pl.BlockSpec((1, 1, 3 * D_MODEL), lambda b, t: (mod_row_fn(b), 0, 0))
    gct_spec = pl.BlockSpec((1, AUX_GROUP, tm), lambda b, t: (b, 0, t))
    sds = lambda w, dt: jax.ShapeDtypeStruct((bsz, n, w), dt)
    gct_sds = jax.ShapeDtypeStruct((bsz, AUX_GROUP, n), F32)
    tail_in = [lvec, cvec, lblk, ublk]
    tail_specs = [_const_spec(a.shape) for a in tail_in]
    if latent:
        er, ec = tables
        ins = [xin, mod3, er, ec, wf, wqkv, wg, wr, wab, wad, wdt] + tail_in
        in_specs = ([x_spec, mod_spec] + [_const_spec(a.shape) for a in ins[2:11]] + tail_specs)
        out_specs = [tok(F_WIDTH), tok(F_WIDTH), tok(3 * DN_WIDTH), tok(DN_WIDTH), tok(D_MODEL),
                     tok(D_MODEL), tok(LANES), gct_spec]
        out_shape = [sds(F_WIDTH, BF16), sds(F_WIDTH, BF16), sds(3 * DN_WIDTH, BF16), sds(DN_WIDTH, BF16),
                     sds(D_MODEL, BF16), sds(D_MODEL, BF16), sds(LANES, F32), gct_sds]
    else:
        ins = [xin, mod3, wqkv, wab, wad, wdt] + tail_in
        in_specs = ([x_spec, mod_spec] + [_const_spec(a.shape) for a in ins[2:6]] + tail_specs)
        out_specs = [tok(3 * DN_WIDTH), tok(LANES), gct_spec]
        out_shape = [sds(3 * DN_WIDTH, BF16), sds(LANES, F32), gct_sds]
    return pl.pallas_call(
        functools.partial(_inproj_kernel, tm=tm, latent=latent),
        grid=(bsz, nt),
        in_specs=in_specs,
        out_specs=out_specs,
        out_shape=out_shape,
        compiler_params=pltpu.CompilerParams(dimension_semantics=("arbitrary", "arbitrary"),
                                             vmem_limit_bytes=VMEM_LIMIT),
        name="inproj_latent" if latent else "inproj_ctx",
    )(*ins)


def _conv_silu(ref, w_ref, c, nc):
    r0 = pl.multiple_of(c * CHUNK, CHUNK)
    cur = ref[0, pl.ds(r0, CHUNK), :].astype(F32)
    p0 = pl.multiple_of(jnp.maximum(r0 - 16, 0), 16)
    n0 = pl.multiple_of(jnp.minimum(r0 + CHUNK, (nc - 1) * CHUNK + CHUNK - 16), 16)
    prev = ref[0, pl.ds(p0, 16), :].astype(F32)[15:16, :] * (c > 0).astype(F32)
    nxt = ref[0, pl.ds(n0, 16), :].astype(F32)[0:1, :] * (c < nc - 1).astype(F32)
    row = lax.broadcasted_iota(jnp.int32, (CHUNK, LANES), 0)
    xm1 = jnp.where(row == 0, prev, pltpu.roll(cur, 1, axis=0))
    xp1 = jnp.where(row == CHUNK - 1, nxt, pltpu.roll(cur, CHUNK - 1, axis=0))
    w = w_ref[...]
    return _silu(xm1 * w[0:1, :] + cur * w[1:2, :] + xp1 * w[2:3, :])


def _block_diag2(m):
    lane = lax.broadcasted_iota(jnp.int32, m.shape, 1)
    z = jnp.zeros_like(m)
    return jnp.concatenate([jnp.where(lane < CHUNK, m, z), jnp.where(lane >= CHUNK, m, z)], axis=0).astype(BF16)


def _dn_masks():
    ri = lax.broadcasted_iota(jnp.int32, (CHUNK, 2 * CHUNK), 0)
    lane = lax.broadcasted_iota(jnp.int32, (CHUNK, 2 * CHUNK), 1)
    fwd = lane < CHUNK
    ci = jnp.where(fwd, lane, lane - CHUNK)
    bwd = jnp.logical_not(fwd)
    incl = (fwd & (ri >= ci)) | (bwd & (ri <= ci))
    strict = (fwd & (ri > ci)) | (bwd & (ri < ci))
    same = lambda k: (ri >> k) == (ci >> k)
    levels = [same(k + 1) & jnp.logical_not(same(k)) for k in range(1, int(math.log2(CHUNK)))]
    eye = jnp.where(ri == ci, 1.0, 0.0)
    return fwd, incl, strict, same(1), levels, eye


def _dn_prepare(q_ref, k_ref, v_ref, aux_ref, gt_ref, cw_refs, h, chunks, nc, pos_f, pos_b, scr, o0_refs):
    m_s, b_s, qt_s, egt_s = scr
    fwd, incl, strict, same1, levels, eye = _dn_masks()
    n = len(chunks)
    col = lambda aux, j, d: aux[:, j * AUX_GROUP + d * DN_HEADS:j * AUX_GROUP + d * DN_HEADS + 1]
    qs, ks, vs, auxs, ls, decays, qks = [], [], [], [], [], [], []
    for c in chunks:
        r0 = pl.multiple_of(c * CHUNK, CHUNK)
        q = _conv_silu(q_ref, cw_refs[0], c, nc)
        k = _conv_silu(k_ref, cw_refs[1], c, nc)
        v = _conv_silu(v_ref, cw_refs[2], c, nc)
        q = q * (lax.rsqrt(jnp.sum(q * q, axis=-1, keepdims=True) + EPS) * (DN_HEAD_DIM ** -0.5))
        k = k * lax.rsqrt(jnp.sum(k * k, axis=-1, keepdims=True) + EPS)
        gram = _dot_nt(jnp.concatenate([k, q], axis=0), jnp.concatenate([k, k], axis=0))
        kk, qk = gram[:CHUNK], gram[CHUNK:]
        aux = pltpu.roll(aux_ref[0, pl.ds(r0, CHUNK), :], (LANES - h) % LANES, axis=1)
        both = lambda j: jnp.where(fwd, col(aux, j, 0), col(aux, j, 1))
        gr = gt_ref[0, h, pl.ds(c, 1), :]
        decay = jnp.exp(jnp.where(incl, both(1) - gr, -jnp.inf))
        ls.append(jnp.where(strict, kk * decay * both(0), 0.0))
        qs.append(q); ks.append(k); vs.append(v); auxs.append(aux); decays.append(decay); qks.append(qk)

    xs = [eye - jnp.where(same1, l, 0.0) for l in ls]
    for lvl in levels:
        ts = [jnp.dot(x.astype(BF16), _block_diag2(jnp.where(lvl, l, 0.0)), preferred_element_type=F32)
              for x, l in zip(xs, ls)]
        xs = [x - jnp.dot(t.astype(BF16), _block_diag2(x), preferred_element_type=F32) for x, t in zip(xs, ts)]

    z = jnp.zeros((CHUNK, 2 * DN_HEAD_DIM), BF16)
    sols = []
    for x, k, v, aux in zip(xs, ks, vs, auxs):
        rhs_f = jnp.concatenate([v * col(aux, 0, 0), k * col(aux, 3, 0)], axis=-1).astype(BF16)
        rhs_b = jnp.concatenate([v * col(aux, 0, 1), k * col(aux, 3, 1)], axis=-1).astype(BF16)
        rhs = jnp.concatenate([jnp.concatenate([rhs_f, z], axis=-1), jnp.concatenate([z, rhs_b], axis=-1)], axis=0)
        sols.append(jnp.dot(x.astype(BF16), rhs, preferred_element_type=F32).astype(BF16))
    mbs = []
    for sol, k, aux, qk, decay in zip(sols, ks, auxs, qks, decays):
        uw = jnp.concatenate([jnp.concatenate([sol[:, :2 * DN_HEAD_DIM], z], axis=-1),
                              jnp.concatenate([z, sol[:, 2 * DN_HEAD_DIM:]], axis=-1)], axis=0)
        kdt = jnp.concatenate([k * col(aux, 4, 0), k * col(aux, 4, 1)], axis=0).T
        lhs = jnp.concatenate([kdt, qk * decay], axis=0).astype(BF16)
        mbs.append(jnp.dot(lhs, uw, preferred_element_type=F32))
    w = DN_HEAD_DIM
    for i in range(n):
        mb, aux = mbs[i], auxs[i]
        r0 = pl.multiple_of(chunks[i] * CHUNK, CHUNK)
        for d, pos in ((0, pos_f[i]), (1, pos_b[i])):
            b_s[d, pos] = mb[:w, (2 * d) * w:(2 * d + 1) * w]
            m_s[d, pos] = mb[:w, (2 * d + 1) * w:(2 * d + 2) * w].astype(BF16)
            egt_s[d, pos] = jnp.broadcast_to(col(aux, 5, d)[0:1, :], (8, LANES))
            if o0_refs is not None:
                qt_s[d, pos] = (qs[i] * col(aux, 2, d) - mb[w:, (2 * d + 1) * w:(2 * d + 2) * w]).astype(BF16)
                o0_refs[d][pl.ds(r0, CHUNK), :] = mb[w:, (2 * d) * w:(2 * d + 1) * w]


def _dn_step(d, pos, scr, s_ref):
    m_s, b_s, qt_s, egt_s = scr
    s = s_ref[d]
    sb = s.astype(BF16)
    ms = jnp.dot(m_s[d, pos], sb, preferred_element_type=F32)
    egt = egt_s[d, pos]
    s3 = s.reshape(DN_HEAD_DIM // 8, 8, LANES) * egt[None]
    s_ref[d] = s3.reshape(DN_HEAD_DIM, LANES) - ms + b_s[d, pos]
    return sb


_DN_PREP_GROUP = 8
_DN_FINISH_ROWS = 256


def _dn_kernel(qx, kx, vx, auxx, gtx, qc, kc, vc, auxc, gtc, cwq, cwk, cwv, sg_ref, nw_ref, out_ref,
               m_s, b_s, qt_s, egt_s, of_s, ob_s, s_ref, *, ncx, ncc):
    h = pl.program_id(1)
    scr = (m_s, b_s, qt_s, egt_s)
    cws = (cwq, cwk, cwv)
    nct = ncx + ncc
    gx = min(_DN_PREP_GROUP, ncx)
    gc = min(_DN_PREP_GROUP, ncc)

    def prep_ctx(i, carry):
        cs = [i * gc + j for j in range(gc)]
        _dn_prepare(qc, kc, vc, auxc, gtc, cws, h, cs, ncc, cs, [ncc - 1 - c for c in cs], scr, None)
        return carry

    def prep_lat(i, carry):
        cs = [i * gx + j for j in range(gx)]
        _dn_prepare(qx, kx, vx, auxx, gtx, cws, h, cs, ncx, [ncc + c for c in cs], [nct - 1 - c for c in cs],
                    scr, (of_s, ob_s))
        return carry

    lax.fori_loop(0, ncc // gc, prep_ctx, 0)
    lax.fori_loop(0, ncx // gx, prep_lat, 0)

    s_ref[...] = jnp.zeros_like(s_ref)

    def scan_ctx(i, carry):
        _dn_step(0, i, scr, s_ref)
        _dn_step(1, i, scr, s_ref)
        return carry

    def scan_lat(i, carry):
        pos = ncc + i
        for d, ref, chunk in ((0, of_s, i), (1, ob_s, ncx - 1 - i)):
            sb = _dn_step(d, pos, scr, s_ref)
            rows = pl.ds(pl.multiple_of(chunk * CHUNK, CHUNK), CHUNK)
            ref[rows, :] = ref[rows, :] + jnp.dot(qt_s[d, pos], sb, preferred_element_type=F32)
        return carry

    lax.fori_loop(0, ncc, scan_ctx, 0)
    lax.fori_loop(0, ncx, scan_lat, 0)

    nw = nw_ref[...]
    fr = min(_DN_FINISH_ROWS, ncx * CHUNK)

    def finish(i, carry):
        rows = pl.ds(pl.multiple_of(i * fr, fr), fr)
        o = of_s[rows, :] + ob_s[rows, :]
        o = o * lax.rsqrt(jnp.mean(o * o, axis=-1, keepdims=True) + EPS) * nw
        out_ref[0, rows, :] = (o * sg_ref[0, rows, :].astype(F32)).astype(BF16)
        return carry

    lax.fori_loop(0, ncx * CHUNK // fr, finish, 0)


def _deltanet(qkv_x, aux_x, gct_x, qkv_c, aux_c, gct_c, conv_w, sg, norm_w):
    bsz, n, _ = qkv_x.shape
    nctx = qkv_c.shape[1]
    ncx, ncc = n // CHUNK, nctx // CHUNK
    nct = ncx + ncc
    assert ncx % min(_DN_PREP_GROUP, ncx) == 0 and ncc % min(_DN_PREP_GROUP, ncc) == 0

    def rows_by_head(g, nc):
        g = g.reshape(bsz, N_DIR, DN_HEADS, nc, CHUNK)
        return jnp.transpose(g, (0, 2, 3, 1, 4)).reshape(bsz, DN_HEADS, nc, N_DIR * CHUNK)

    gct_x = rows_by_head(gct_x, ncx)
    gct_c = rows_by_head(gct_c, ncc)
    col = lambda rows, off: pl.BlockSpec((1, rows, DN_HEAD_DIM), lambda b, h: (b, 0, off + h))
    whole = lambda a: pl.BlockSpec((1,) + a.shape[1:], lambda b, h: (b,) + (0,) * (a.ndim - 1))
    cw = lambda off: pl.BlockSpec((3, DN_HEAD_DIM), lambda b, h: (0, off + h))
    in_specs = [col(n, 0), col(n, DN_HEADS), col(n, 2 * DN_HEADS), whole(aux_x), whole(gct_x),
                col(nctx, 0), col(nctx, DN_HEADS), col(nctx, 2 * DN_HEADS), whole(aux_c), whole(gct_c),
                cw(0), cw(DN_HEADS), cw(2 * DN_HEADS),
                col(n, 0), pl.BlockSpec((1, DN_HEAD_DIM), lambda b, h: (0, 0))]
    scratch = [pltpu.VMEM((N_DIR, nct, DN_HEAD_DIM, DN_HEAD_DIM), BF16),
               pltpu.VMEM((N_DIR, nct, DN_HEAD_DIM, DN_HEAD_DIM), F32),
               pltpu.VMEM((N_DIR, nct, CHUNK, DN_HEAD_DIM), BF16),
               pltpu.VMEM((N_DIR, nct, 8, LANES), F32),
               pltpu.VMEM((n, DN_HEAD_DIM), F32),
               pltpu.VMEM((n, DN_HEAD_DIM), F32),
               pltpu.VMEM((N_DIR, DN_HEAD_DIM, DN_HEAD_DIM), F32)]
    return pl.pallas_call(
        functools.partial(_dn_kernel, ncx=ncx, ncc=ncc),
        grid=(bsz, DN_HEADS),
        in_specs=in_specs,
        out_specs=col(n, 0),
        out_shape=jax.ShapeDtypeStruct((bsz, n, DN_WIDTH), BF16),
        scratch_shapes=scratch,
        compiler_params=pltpu.CompilerParams(dimension_semantics=("arbitrary", "arbitrary"),
                                             vmem_limit_bytes=VMEM_LIMIT),
        name="deltanet",
    )(qkv_x, qkv_x, qkv_x, aux_x, gct_x, qkv_c, qkv_c, qkv_c, aux_c, gct_c,
      conv_w, conv_w, conv_w, sg, norm_w)


_F_ROWS = 128


def _fourier_kernel(fv_ref, m_ref, r_ref, out_ref, pq_s, *, n4):
    k1 = pl.program_id(1)
    rb = min(_F_ROWS, n4)

    def combo(x, k):
        if k == 0:
            return x[0] + x[1] + x[2] + x[3], None
        if k == 2:
            return x[0] - x[1] + x[2] - x[3], None
        if k == 1:
            return x[0] - x[2], x[3] - x[1]
        return x[0] - x[2], x[1] - x[3]

    r = r_ref[...]
    for k in range(4):
        @pl.when(k1 == k)
        def _():
            def body(i, carry):
                r0 = pl.multiple_of(i * rb, rb)
                x = [fv_ref[0, pl.ds(j * n4 + r0, rb), :].astype(F32) for j in range(4)]
                d, e = combo(x, k)
                for g in range(F_GROUPS):
                    sl = slice(g * F_GROUP_DIM, (g + 1) * F_GROUP_DIM)
                    if e is None:
                        pq = jnp.dot(d[:, sl].astype(BF16), r[:F_GROUP_DIM, :], preferred_element_type=F32)
                    else:
                        de = jnp.concatenate([d[:, sl], e[:, sl]], axis=-1).astype(BF16)
                        pq = jnp.dot(de, r, preferred_element_type=F32)
                    pq_s[pl.ds(r0, rb), sl] = pq[:, :F_GROUP_DIM].astype(BF16)
                    pq_s[pl.ds(n4 + r0, rb), sl] = pq[:, F_GROUP_DIM:].astype(BF16)
                return carry
            lax.fori_loop(0, n4 // rb, body, 0)

    def rows(i, carry):
        r0 = pl.multiple_of(i * rb, rb)
        res = jnp.dot(m_ref[0, pl.ds(r0, rb), :], pq_s[...], preferred_element_type=F32)
        for g in range(F_GROUPS):
            out_ref[0, g, pl.ds(k1 + 4 * r0, rb, stride=4), :] = res[:, g * F_GROUP_DIM:(g + 1) * F_GROUP_DIM]
        return carry
    lax.fori_loop(0, n4 // rb, rows, 0)


def _fourier(fv, mcat, rmat):
    bsz, n, _ = fv.shape
    n4 = n // 4
    return pl.pallas_call(
        functools.partial(_fourier_kernel, n4=n4),
        grid=(bsz, 4),
        in_specs=[pl.BlockSpec((1, n, F_WIDTH), lambda b, k: (b, 0, 0)),
                  pl.BlockSpec((1, n4, 2 * n4), lambda b, k: (k, 0, 0)),
                  pl.BlockSpec((2 * F_GROUP_DIM, 2 * F_GROUP_DIM), lambda b, k: (0, 0))],
        out_specs=pl.BlockSpec((1, F_GROUPS, n, F_GROUP_DIM), lambda b, k: (b, 0, 0, 0)),
        out_shape=jax.ShapeDtypeStruct((bsz, F_GROUPS, n, F_GROUP_DIM), F32),
        scratch_shapes=[pltpu.VMEM((2 * n4, F_WIDTH), BF16)],
        compiler_params=pltpu.CompilerParams(dimension_semantics=("arbitrary", "arbitrary"),
                                             vmem_limit_bytes=VMEM_LIMIT),
        name="fourier",
    )(fv, mcat, rmat)


def _merge_kernel(x_ref, mod_ref, er_ref, ec_ref, mix_ref, fg_ref, og_ref, rf_ref, rd_ref,
                  wfm_ref, wfo_ref, wdo_ref, wo_ref, lng_ref, lnb_ref, out_ref, *, tm, alpha):
    t = pl.program_id(1)
    mixed = jnp.concatenate(
        [jnp.dot(mix_ref[0, g].astype(BF16), wfm_ref[g], preferred_element_type=F32)
         for g in range(F_GROUPS)], axis=-1)
    y_f = jnp.dot((mixed * fg_ref[0].astype(F32)).astype(BF16), wfo_ref[...], preferred_element_type=F32)
    y_d = jnp.dot(og_ref[0], wdo_ref[...], preferred_element_type=F32)
    m = rf_ref[0].astype(F32) * y_f + rd_ref[0].astype(F32) * y_d
    o = jnp.dot(m.astype(BF16), wo_ref[...], preferred_element_type=F32)
    gate = mod_ref[0][:, 2 * D_MODEL:3 * D_MODEL]
    x = _add_pos(x_ref[0], er_ref, ec_ref, t, tm)
    out_ref[0] = _layer_norm(alpha * x + gate * o) * lng_ref[...] + lnb_ref[...]


def _merge(x, mod3, er, ec, mixed, fg, og, rf, rd, wfm, wfo, wdo, wo, ln_g, ln_b, *, tm, alpha):
    bsz, n, _ = x.shape
    tok = lambda w: pl.BlockSpec((1, tm, w), lambda b, t: (b, t, 0))
    consts = [wfm, wfo, wdo, wo, ln_g, ln_b]
    in_specs = [tok(D_MODEL),
                pl.BlockSpec((1, 1, 3 * D_MODEL), lambda b, t: (b, 0, 0)),
                _const_spec(er.shape), _const_spec(ec.shape),
                pl.BlockSpec((1, F_GROUPS, tm, F_GROUP_DIM), lambda b, t: (b, 0, t, 0)),
                tok(F_WIDTH), tok(DN_WIDTH), tok(D_MODEL), tok(D_MODEL)]
    in_specs += [_const_spec(a.shape) for a in consts]
    return pl.pallas_call(
        functools.partial(_merge_kernel, tm=tm, alpha=alpha),
        grid=(bsz, n // tm),
        in_specs=in_specs,
        out_specs=tok(D_MODEL),
        out_shape=jax.ShapeDtypeStruct((bsz, n, D_MODEL), F32),
        compiler_params=pltpu.CompilerParams(dimension_semantics=("arbitrary", "arbitrary"),
                                             vmem_limit_bytes=VMEM_LIMIT),
        name="merge",
    )(x, mod3, er, ec, mixed, fg, og, rf, rd, *consts)


def _pos_tables(rows):
    quarter = D_MODEL // 4
    omega = 1.0 / (10000.0 ** (np.arange(quarter, dtype=np.float64) / quarter))
    pr = np.arange(rows, dtype=np.float64)[:, None] * omega
    pc = np.arange(GRID_W, dtype=np.float64)[:, None] * omega
    er = np.concatenate([np.sin(pr), np.cos(pr)], axis=-1)
    ec = np.concatenate([np.sin(pc), np.cos(pc)], axis=-1)
    return jnp.asarray(er, F32), jnp.asarray(ec, F32)


def _chunk_sum_matrices(tm):
    i = np.arange(tm)
    same = (i[:, None] // CHUNK) == (i[None, :] // CHUNK)
    lower = same & (i[:, None] >= i[None, :])
    upper = same & (i[:, None] <= i[None, :])
    return jnp.asarray(lower, BF16), jnp.asarray(upper, BF16)


def _channel_dft_matrix():
    c = np.arange(F_GROUP_DIM)
    ang = 2.0 * np.pi * ((c[:, None] * c[None, :]) % F_GROUP_DIM) / F_GROUP_DIM
    cc, sc = np.cos(ang), np.sin(ang)
    r = np.block([[cc, sc], [sc, -cc]]) / math.sqrt(F_GROUP_DIM)
    return jnp.asarray(r, F32).astype(BF16)


def _position_dft_matrices(n):
    n4 = n // 4
    k1 = jnp.arange(4, dtype=jnp.int32)[:, None, None]
    k2 = jnp.arange(n4, dtype=jnp.int32)[None, :, None]
    n2 = jnp.arange(n4, dtype=jnp.int32)[None, None, :]
    ang = ((n2 * (k1 + 4 * k2)) % n).astype(F32) * (2.0 * math.pi / n)
    scale = 1.0 / math.sqrt(n)
    return jnp.concatenate([jnp.cos(ang) * scale, -jnp.sin(ang) * scale], axis=-1).astype(BF16)


def _aux_weights(w_beta, w_decay, a_log, dt_bias):
    reps = LANES // AUX_GROUP
    wab = jnp.tile(w_beta, (1, reps)).astype(BF16)
    wad = jnp.tile(w_decay, (1, reps)).astype(BF16)
    wdt = w_decay.T.astype(BF16)
    al = a_log.reshape(1, AUX_GROUP).astype(F32)
    db = dt_bias.reshape(1, AUX_GROUP).astype(F32)
    lvec = jnp.concatenate([jnp.tile(al, (1, reps)), jnp.tile(db, (1, reps)),
                            jnp.zeros((6, LANES), F32)], axis=0)
    cvec = jnp.concatenate([al.T, db.T], axis=1)
    return wab, wad, wdt, lvec, cvec


def kernel(x, c, ctx, c_ctx, w_mod, b_mod, w_in, conv_w, a_log, dt_bias, dn_norm_w, w_dn_out, w_fmix,
           w_f_out, w_out, ln_g, ln_b):
    depth = w_mod.shape[0]
    assert depth == 1, "single-layer configuration"
    bsz, n, _ = x.shape
    nctx = ctx.shape[1]
    assert n % (4 * CHUNK) == 0 and nctx % CHUNK == 0 and bsz <= 8
    alpha = (2 * depth) ** 0.25
    tm = min(512, n)
    tmc = min(256, nctx)

    cc = jnp.zeros((16, D_MODEL), F32).at[:bsz].set(c).at[8].set(c_ctx)
    mod3 = _modulation(cc, w_mod[0], b_mod[0].reshape(1, -1)).reshape(16, 1, 3 * D_MODEL)

    w = w_in[0]
    o0, o1, o2, o3, o4, o5, o6 = (int(v) for v in np.cumsum(
        (F_WIDTH, F_WIDTH, 3 * DN_WIDTH, DN_WIDTH, AUX_GROUP, AUX_GROUP, D_MODEL)))
    wf = w[:, :o1].astype(BF16)
    wqkv = w[:, o1:o2].astype(BF16)
    wg = w[:, o2:o3].astype(BF16)
    wr = w[:, o5:].astype(BF16)
    wab, wad, wdt, lvec, cvec = _aux_weights(w[:, o3:o4], w[:, o4:o5], a_log[0], dt_bias[0])
    er, ec = _pos_tables(n // GRID_W)

    lblk, ublk = _chunk_sum_matrices(tm)
    fv, fg, qkv_x, sg, rf, rd, aux_x, gct_x = _inproj(
        x, mod3, lambda b: b, (er, ec), (wf, wqkv, wg, wr, wab, wad, wdt), (lvec, cvec, lblk, ublk),
        tm=tm, latent=True)
    lblk_c, ublk_c = _chunk_sum_matrices(tmc)
    qkv_c, aux_c, gct_c = _inproj(
        ctx, mod3, lambda b: 8, None, (wf, wqkv, wg, wr, wab, wad, wdt), (lvec, cvec, lblk_c, ublk_c),
        tm=tmc, latent=False)

    og = _deltanet(qkv_x, aux_x, gct_x, qkv_c, aux_c, gct_c, conv_w[0], sg,
                   dn_norm_w[0].reshape(1, -1).astype(F32))
    mixed = _fourier(fv, _position_dft_matrices(n), _channel_dft_matrix())
    return _merge(x, mod3, er, ec, mixed, fg, og, rf, rd,
                  w_fmix[0].astype(BF16), w_f_out[0].astype(BF16), w_dn_out[0].astype(BF16),
                  w_out[0].astype(BF16), ln_g[0].reshape(1, -1), ln_b[0].reshape(1, -1),
                  tm=tm, alpha=alpha)
```

```python
import functools
import math

import numpy as np
import jax
import jax.numpy as jnp
from jax import lax
from jax.experimental import pallas as pl
from jax.experimental.pallas import tpu as pltpu

F32 = jnp.float32
BF16 = jnp.bfloat16

D_MODEL = 1024
GRID_W = 64
F_GROUPS = 4
F_GROUP_DIM = 128
F_WIDTH = F_GROUPS * F_GROUP_DIM
DN_HEADS = 8
DN_HEAD_DIM = 128
DN_WIDTH = DN_HEADS * DN_HEAD_DIM
N_DIR = 2
CHUNK = 64
EPS = 1e-6
LANES = 128
AUX_GROUP = 16
VMEM_LIMIT = 56 * 1024 * 1024


def _silu(x):
    return x * jax.nn.sigmoid(x)


def _softplus(x):
    return jnp.maximum(x, 0.0) + jnp.log1p(jnp.exp(-jnp.abs(x)))


def _dot(a, b):
    return jnp.dot(a.astype(BF16), b.astype(BF16), preferred_element_type=F32)


def _dot_nt(a, b):
    return lax.dot_general(a.astype(BF16), b.astype(BF16), (((1,), (1,)), ((), ())),
                           preferred_element_type=F32)


def _split3(x):
    hi = x.astype(BF16)
    r1 = x - hi.astype(F32)
    mid = r1.astype(BF16)
    lo = (r1 - mid.astype(F32)).astype(BF16)
    return hi, mid, lo


def _layer_norm(x):
    mu = jnp.mean(x, axis=-1, keepdims=True)
    xc = x - mu
    var = jnp.mean(xc * xc, axis=-1, keepdims=True)
    return xc * lax.rsqrt(var + EPS)


def _add_pos(x, er_ref, ec_ref, t, tm):
    rows = tm // GRID_W
    half = D_MODEL // 2
    er = er_ref[pl.ds(t * rows, rows), :]
    ec = ec_ref[...]
    x3 = x.reshape(rows, GRID_W, D_MODEL)
    pe = jnp.concatenate([jnp.broadcast_to(er[:, None, :], (rows, GRID_W, half)),
                          jnp.broadcast_to(ec[None, :, :], (rows, GRID_W, half))], axis=-1)
    return (x3 + pe).reshape(tm, D_MODEL)


def _mod_kernel(c_ref, w_ref, b_ref, o_ref):
    s = _silu(c_ref[...])
    o_ref[...] = jnp.dot(s, w_ref[...], preferred_element_type=F32,
                         precision=lax.Precision.HIGHEST) + b_ref[...]


def _modulation(cc, w_mod, b_mod):
    rows = cc.shape[0]
    nblk = 3
    return pl.pallas_call(
        _mod_kernel,
        grid=(nblk,),
        in_specs=[pl.BlockSpec((rows, D_MODEL), lambda j: (0, 0)),
                  pl.BlockSpec((D_MODEL, D_MODEL), lambda j: (0, j)),
                  pl.BlockSpec((1, D_MODEL), lambda j: (0, j))],
        out_specs=pl.BlockSpec((rows, D_MODEL), lambda j: (0, j)),
        out_shape=jax.ShapeDtypeStruct((rows, 3 * D_MODEL), F32),
        compiler_params=pltpu.CompilerParams(dimension_semantics=("arbitrary",),
                                             vmem_limit_bytes=VMEM_LIMIT),
        name="mod",
    )(cc, w_mod, b_mod)


def _aux_from_raw(raw_b, raw_d, alog_l, dtb_l, lblk, tm):
    beta = jax.nn.sigmoid(raw_b)
    g = -jnp.exp(alog_l) * _softplus(raw_d + dtb_l)
    hi, mid, lo = _split3(g)
    pre = (jnp.dot(lblk, hi, preferred_element_type=F32)
           + jnp.dot(lblk, mid, preferred_element_type=F32)
           + jnp.dot(lblk, lo, preferred_element_type=F32))
    nck = tm // CHUNK
    pre3 = pre.reshape(nck, CHUNK, LANES)
    tot = jnp.broadcast_to(pre3[:, CHUNK - 1:CHUNK, :], (nck, CHUNK, LANES)).reshape(tm, LANES)
    suf = tot - pre + g
    lane = lax.broadcasted_iota(jnp.int32, (tm, LANES), 1)
    backward = (lane % AUX_GROUP) >= DN_HEADS
    gc = jnp.where(backward, suf, pre)
    egc = jnp.exp(gc)
    grp = lane // AUX_GROUP
    aux = jnp.where(grp == 0, beta,
          jnp.where(grp == 1, gc,
          jnp.where(grp == 2, egc,
          jnp.where(grp == 3, beta * egc,
          jnp.where(grp == 4, jnp.exp(tot - gc),
          jnp.where(grp == 5, jnp.exp(tot), 0.0))))))
    return aux


def _gct_from_raw(raw_t, alog_c, dtb_c, lblk, ublk):
    g = -jnp.exp(alog_c) * _softplus(raw_t + dtb_c)
    hi, mid, lo = _split3(g)
    pre = (jnp.dot(hi, ublk, preferred_element_type=F32)
           + jnp.dot(mid, ublk, preferred_element_type=F32)
           + jnp.dot(lo, ublk, preferred_element_type=F32))
    suf = (jnp.dot(hi, lblk, preferred_element_type=F32)
           + jnp.dot(mid, lblk, preferred_element_type=F32)
           + jnp.dot(lo, lblk, preferred_element_type=F32))
    row = lax.broadcasted_iota(jnp.int32, g.shape, 0)
    return jnp.where(row >= DN_HEADS, suf, pre)


def _inproj_kernel(*refs, tm, latent):
    if latent:
        (x_ref, mod_ref, er_ref, ec_ref, wf_ref, wqkv_ref, wg_ref, wr_ref, wab_ref, wad_ref, wdt_ref,
         lvec_ref, cvec_ref, lblk_ref, ublk_ref,
         fv_ref, fg_ref, qkv_ref, sg_ref, rf_ref, rd_ref, aux_ref, gct_ref) = refs
    else:
        (x_ref, mod_ref, wqkv_ref, wab_ref, wad_ref, wdt_ref,
         lvec_ref, cvec_ref, lblk_ref, ublk_ref,
         qkv_ref, aux_ref, gct_ref) = refs
    t = pl.program_id(1)
    x = x_ref[0]
    if latent:
        x = _add_pos(x, er_ref, ec_ref, t, tm)
    mod = mod_ref[0]
    shift = mod[:, 0:D_MODEL]
    scale = mod[:, D_MODEL:2 * D_MODEL]
    h = (_layer_norm(x) * (1.0 + scale) + shift).astype(BF16)

    qkv_ref[0] = jnp.dot(h, wqkv_ref[...], preferred_element_type=F32).astype(BF16)
    if latent:
        pf = jnp.dot(h, wf_ref[...], preferred_element_type=F32)
        fv_ref[0] = pf[:, :F_WIDTH].astype(BF16)
        fg_ref[0] = _silu(pf[:, F_WIDTH:]).astype(BF16)
        sg_ref[0] = _silu(jnp.dot(h, wg_ref[...], preferred_element_type=F32)).astype(BF16)
        pr = jnp.dot(h, wr_ref[...], preferred_element_type=F32)
        rf_ref[0] = jax.nn.sigmoid(pr[:, :D_MODEL]).astype(BF16)
        rd_ref[0] = jax.nn.sigmoid(pr[:, D_MODEL:]).astype(BF16)

    lblk = lblk_ref[...]
    raw_b = jnp.dot(h, wab_ref[...], preferred_element_type=F32)
    raw_d = jnp.dot(h, wad_ref[...], preferred_element_type=F32)
    aux_ref[0] = _aux_from_raw(raw_b, raw_d, lvec_ref[0:1, :], lvec_ref[1:2, :], lblk, tm)
    raw_t = lax.dot_general(wdt_ref[...], h, (((1,), (1,)), ((), ())), preferred_element_type=F32)
    gct_ref[0] = _gct_from_raw(raw_t, cvec_ref[:, 0:1], cvec_ref[:, 1:2], lblk, ublk_ref[...])


def _const_spec(shape):
    nd = len(shape)
    return pl.BlockSpec(shape, lambda b, t: (0,) * nd)


def _inproj(xin, mod3, mod_row_fn, tables, weights, consts, *, tm, latent):
    bsz, n, _ = xin.shape
    nt = n // tm
    wf, wqkv, wg, wr, wab, wad, wdt = weights
    lvec, cvec, lblk, ublk = consts
    tok = lambda w: pl.BlockSpec((1, tm, w), lambda b, t: (b, t, 0))
    x_spec = pl.BlockSpec((1, tm, D_MODEL), lambda b, t: (b, t, 0))
    mod_spec = pl.BlockSpec((1, 1, 3 * D_MODEL), lambda b, t: (mod_row_fn(b), 0, 0))
    gct_spec = pl.BlockSpec((1, AUX_GROUP, tm), lambda b, t: (b, 0, t))
    sds = lambda w, dt: jax.ShapeDtypeStruct((bsz, n, w), dt)
    gct_sds = jax.ShapeDtypeStruct((bsz, AUX_GROUP, n), F32)
    tail_in = [lvec, cvec, lblk, ublk]
    tail_specs = [_const_spec(a.shape) for a in tail_in]
    if latent:
        er, ec = tables
        ins = [xin, mod3, er, ec, wf, wqkv, wg, wr, wab, wad, wdt] + tail_in
        in_specs = ([x_spec, mod_spec] + [_const_spec(a.shape) for a in ins[2:11]] + tail_specs)
        out_specs = [tok(F_WIDTH), tok(F_WIDTH), tok(3 * DN_WIDTH), tok(DN_WIDTH), tok(D_MODEL),
                     tok(D_MODEL), tok(LANES), gct_spec]
        out_shape = [sds(F_WIDTH, BF16), sds(F_WIDTH, BF16), sds(3 * DN_WIDTH, BF16), sds(DN_WIDTH, BF16),
                     sds(D_MODEL, BF16), sds(D_MODEL, BF16), sds(LANES, F32), gct_sds]
    else:
        ins = [xin, mod3, wqkv, wab, wad, wdt] + tail_in
        in_specs = ([x_spec, mod_spec] + [_const_spec(a.shape) for a in ins[2:6]] + tail_specs)
        out_specs = [tok(3 * DN_WIDTH), tok(LANES), gct_spec]
        out_shape = [sds(3 * DN_WIDTH, BF16), sds(LANES, F32), gct_sds]
    return pl.pallas_call(
        functools.partial(_inproj_kernel, tm=tm, latent=latent),
        grid=(bsz, nt),
        in_specs=in_specs,
        out_specs=out_specs,
        out_shape=out_shape,
        compiler_params=pltpu.CompilerParams(dimension_semantics=("arbitrary", "arbitrary"),
                                             vmem_limit_bytes=VMEM_LIMIT),
        name="inproj_latent" if latent else "inproj_ctx",
    )(*ins)


def _conv_silu(ref, w_ref, c, nc):
    r0 = pl.multiple_of(c * CHUNK, CHUNK)
    cur = ref[0, pl.ds(r0, CHUNK), :].astype(F32)
    p0 = pl.multiple_of(jnp.maximum(r0 - 16, 0), 16)
    n0 = pl.multiple_of(jnp.minimum(r0 + CHUNK, (nc - 1) * CHUNK + CHUNK - 16), 16)
    prev = ref[0, pl.ds(p0, 16), :].astype(F32)[15:16, :] * jnp.where(c > 0, 1.0, 0.0)
    nxt = ref[0, pl.ds(n0, 16), :].astype(F32)[0:1, :] * jnp.where(c < nc - 1, 1.0, 0.0)
    row = lax.broadcasted_iota(jnp.int32, (CHUNK, LANES), 0)
    xm1 = jnp.where(row == 0, prev, pltpu.roll(cur, 1, axis=0))
    xp1 = jnp.where(row == CHUNK - 1, nxt, pltpu.roll(cur, CHUNK - 1, axis=0))
    w = w_ref[...]
    return _silu(xm1 * w[0:1, :] + cur * w[1:2, :] + xp1 * w[2:3, :])


def _block_diag2(m):
    lane = lax.broadcasted_iota(jnp.int32, m.shape, 1)
    z = jnp.zeros_like(m)
    return jnp.concatenate([jnp.where(lane < CHUNK, m, z), jnp.where(lane >= CHUNK, m, z)], axis=0).astype(BF16)


def _dn_masks():
    ri = lax.broadcasted_iota(jnp.int32, (CHUNK, 2 * CHUNK), 0)
    lane = lax.broadcasted_iota(jnp.int32, (CHUNK, 2 * CHUNK), 1)
    fwd = lane < CHUNK
    ci = jnp.where(fwd, lane, lane - CHUNK)
    bwd = jnp.logical_not(fwd)
    incl = (fwd & (ri >= ci)) | (bwd & (ri <= ci))
    strict = (fwd & (ri > ci)) | (bwd & (ri < ci))
    same = lambda k: (ri >> k) == (ci >> k)
    levels = [same(k + 1) & jnp.logical_not(same(k)) for k in range(1, int(math.log2(CHUNK)))]
    eye = jnp.where(ri == ci, 1.0, 0.0)
    return fwd, incl, strict, same(1), levels, eye


def _dn_prepare(q_ref, k_ref, v_ref, aux_ref, gt_ref, cw_refs, h, chunks, nc, pos_f, pos_b, scr, o0_refs,
                between=()):
    m_s, b_s, qt_s, egt_s = scr
    fwd, incl, strict, same1, levels, eye = _dn_masks()
    between = list(between)
    n = len(chunks)
    col = lambda aux, j, d: aux[:, j * AUX_GROUP + d * DN_HEADS:j * AUX_GROUP + d * DN_HEADS + 1]
    qs, ks, vs, auxs, ls, decays, qks = [], [], [], [], [], [], []
    for c in chunks:
        r0 = pl.multiple_of(c * CHUNK, CHUNK)
        q = _conv_silu(q_ref, cw_refs[0], c, nc)
        k = _conv_silu(k_ref, cw_refs[1], c, nc)
        v = _conv_silu(v_ref, cw_refs[2], c, nc)
        q = q * (lax.rsqrt(jnp.sum(q * q, axis=-1, keepdims=True) + EPS) * (DN_HEAD_DIM ** -0.5))
        k = k * lax.rsqrt(jnp.sum(k * k, axis=-1, keepdims=True) + EPS)
        gram = _dot_nt(jnp.concatenate([k, q], axis=0), jnp.concatenate([k, k], axis=0))
        kk, qk = gram[:CHUNK], gram[CHUNK:]
        aux = pltpu.roll(aux_ref[0, pl.ds(r0, CHUNK), :], (LANES - h) % LANES, axis=1)
        both = lambda j: jnp.where(fwd, col(aux, j, 0), col(aux, j, 1))
        gr = gt_ref[0, h, pl.ds(c, 1), :]
        decay = jnp.exp(jnp.where(incl, both(1) - gr, -jnp.inf))
        ls.append(jnp.where(strict, kk * decay * both(0), 0.0))
        qs.append(q); ks.append(k); vs.append(v); auxs.append(aux); decays.append(decay); qks.append(qk)

    xs = [eye - jnp.where(same1, l, 0.0) for l in ls]
    for lvl in levels:
        if between:
            between.pop(0)()
        ts = [jnp.dot(x.astype(BF16), _block_diag2(jnp.where(lvl, l, 0.0)), preferred_element_type=F32)
              for x, l in zip(xs, ls)]
        xs = [x - jnp.dot(t.astype(BF16), _block_diag2(x), preferred_element_type=F32) for x, t in zip(xs, ts)]
    while between:
        between.pop(0)()

    z = jnp.zeros((CHUNK, 2 * DN_HEAD_DIM), BF16)
    sols = []
    for x, k, v, aux in zip(xs, ks, vs, auxs):
        rhs_f = jnp.concatenate([v * col(aux, 0, 0), k * col(aux, 3, 0)], axis=-1).astype(BF16)
        rhs_b = jnp.concatenate([v * col(aux, 0, 1), k * col(aux, 3, 1)], axis=-1).astype(BF16)
        rhs = jnp.concatenate([jnp.concatenate([rhs_f, z], axis=-1), jnp.concatenate([z, rhs_b], axis=-1)], axis=0)
        sols.append(jnp.dot(x.astype(BF16), rhs, preferred_element_type=F32).astype(BF16))
    mbs = []
    for sol, k, aux, qk, decay in zip(sols, ks, auxs, qks, decays):
        uw = jnp.concatenate([jnp.concatenate([sol[:, :2 * DN_HEAD_DIM], z], axis=-1),
                              jnp.concatenate([z, sol[:, 2 * DN_HEAD_DIM:]], axis=-1)], axis=0)
        kdt = jnp.concatenate([k * col(aux, 4, 0), k * col(aux, 4, 1)], axis=0).T
        lhs = jnp.concatenate([kdt, qk * decay], axis=0).astype(BF16)
        mbs.append(jnp.dot(lhs, uw, preferred_element_type=F32))
    w = DN_HEAD_DIM
    for i in range(n):
        mb, aux = mbs[i], auxs[i]
        r0 = pl.multiple_of(chunks[i] * CHUNK, CHUNK)
        for d, pos in ((0, pos_f[i]), (1, pos_b[i])):
            b_s[d, pos] = mb[:w, (2 * d) * w:(2 * d + 1) * w]
            m_s[d, pos] = mb[:w, (2 * d + 1) * w:(2 * d + 2) * w].astype(BF16)
            egt_s[d, pos] = jnp.broadcast_to(col(aux, 5, d)[0:1, :], (8, LANES))
            if o0_refs is not None:
                qt_s[d, pos] = (qs[i] * col(aux, 2, d) - mb[w:, (2 * d + 1) * w:(2 * d + 2) * w]).astype(BF16)
                o0_refs[d][pl.ds(r0, CHUNK), :] = mb[w:, (2 * d) * w:(2 * d + 1) * w]


def _dn_step(d, pos, scr, s_ref):
    m_s, b_s, qt_s, egt_s = scr
    s = s_ref[d]
    sb = s.astype(BF16)
    ms = jnp.dot(m_s[d, pos], sb, preferred_element_type=F32)
    egt = egt_s[d, pos]
    s3 = s.reshape(DN_HEAD_DIM // 8, 8, LANES) * egt[None]
    s_ref[d] = s3.reshape(DN_HEAD_DIM, LANES) - ms + b_s[d, pos]
    return sb


_DN_PREP_GROUP = 8
_DN_FINISH_ROWS = 256


def _dn_kernel(qx, kx, vx, auxx, gtx, qc, kc, vc, auxc, gtc, cwq, cwk, cwv, sg_ref, nw_ref, out_ref,
               m_s, b_s, qt_s, egt_s, of_s, ob_s, s_ref, *, ncx, ncc):
    h = pl.program_id(1)
    scr = (m_s, b_s, qt_s, egt_s)
    cws = (cwq, cwk, cwv)
    nct = ncx + ncc
    half = _DN_PREP_GROUP // 2
    ngroups = ncx // _DN_PREP_GROUP

    def ctx_step(i):
        _dn_step(0, i, scr, s_ref)
        _dn_step(1, i, scr, s_ref)

    def lat_step(i):
        pos = ncc + i
        for d, ref, chunk in ((0, of_s, i), (1, ob_s, ncx - 1 - i)):
            sb = _dn_step(d, pos, scr, s_ref)
            rows = pl.ds(pl.multiple_of(chunk * CHUNK, CHUNK), CHUNK)
            ref[rows, :] = ref[rows, :] + jnp.dot(qt_s[d, pos], sb, preferred_element_type=F32)

    def prep_lat(g, between):
        front = [g * half + j for j in range(half)]
        cs = front + [ncx - 1 - c for c in front]
        _dn_prepare(qx, kx, vx, auxx, gtx, cws, h, cs, ncx, [ncc + c for c in cs], [nct - 1 - c for c in cs],
                    scr, (of_s, ob_s), between)

    def lat_steps(g):
        return [functools.partial(lat_step, g * half + j) for j in range(half)]

    cs = list(range(ncc))
    _dn_prepare(qc, kc, vc, auxc, gtc, cws, h, cs, ncc, cs, [ncc - 1 - c for c in cs], scr, None)
    s_ref[...] = jnp.zeros_like(s_ref)
    prep_lat(0, [functools.partial(ctx_step, i) for i in range(ncc)])

    def body(g, carry):
        prep_lat(g, lat_steps(g - 1))
        return carry

    lax.fori_loop(1, ngroups, body, 0)
    for step in lat_steps(ngroups - 1):
        step()

    nw = nw_ref[...]

    def finish_chunk(chunk):
        rows = pl.ds(pl.multiple_of(chunk * CHUNK, CHUNK), CHUNK)
        o = of_s[rows, :] + ob_s[rows, :]
        o = o * lax.rsqrt(jnp.mean(o * o, axis=-1, keepdims=True) + EPS) * nw
        out_ref[0, rows, :] = (o * sg_ref[0, rows, :].astype(F32)).astype(BF16)

    def tail(i, carry):
        finish_chunk(i - 1)
        finish_chunk(ncx - i)
        lat_step(i)
        return carry

    first = ngroups * half
    lat_step(first)
    lax.fori_loop(first + 1, ncx, tail, 0)
    finish_chunk(ncx - 1)
    finish_chunk(0)


def _deltanet(qkv_x, aux_x, gct_x, qkv_c, aux_c, gct_c, conv_w, sg, norm_w):
    bsz, n, _ = qkv_x.shape
    nctx = qkv_c.shape[1]
    ncx, ncc = n // CHUNK, nctx // CHUNK
    nct = ncx + ncc
    assert ncx % _DN_PREP_GROUP == 0 and ncc <= _DN_PREP_GROUP

    def rows_by_head(g, nc):
        g = g.reshape(bsz, N_DIR, DN_HEADS, nc, CHUNK)
        return jnp.transpose(g, (0, 2, 3, 1, 4)).reshape(bsz, DN_HEADS, nc, N_DIR * CHUNK)

    gct_x = rows_by_head(gct_x, ncx)
    gct_c = rows_by_head(gct_c, ncc)
    col = lambda rows, off: pl.BlockSpec((1, rows, DN_HEAD_DIM), lambda b, h: (b, 0, off + h))
    whole = lambda a: pl.BlockSpec((1,) + a.shape[1:], lambda b, h: (b,) + (0,) * (a.ndim - 1))
    cw = lambda off: pl.BlockSpec((3, DN_HEAD_DIM), lambda b, h: (0, off + h))
    in_specs = [col(n, 0), col(n, DN_HEADS), col(n, 2 * DN_HEADS), whole(aux_x), whole(gct_x),
                col(nctx, 0), col(nctx, DN_HEADS), col(nctx, 2 * DN_HEADS), whole(aux_c), whole(gct_c),
                cw(0), cw(DN_HEADS), cw(2 * DN_HEADS),
                col(n, 0), pl.BlockSpec((1, DN_HEAD_DIM), lambda b, h: (0, 0))]
    scratch = [pltpu.VMEM((N_DIR, nct, DN_HEAD_DIM, DN_HEAD_DIM), BF16),
               pltpu.VMEM((N_DIR, nct, DN_HEAD_DIM, DN_HEAD_DIM), F32),
               pltpu.VMEM((N_DIR, nct, CHUNK, DN_HEAD_DIM), BF16),
               pltpu.VMEM((N_DIR, nct, 8, LANES), F32),
               pltpu.VMEM((n, DN_HEAD_DIM), F32),
               pltpu.VMEM((n, DN_HEAD_DIM), F32),
               pltpu.VMEM((N_DIR, DN_HEAD_DIM, DN_HEAD_DIM), F32)]
    return pl.pallas_call(
        functools.partial(_dn_kernel, ncx=ncx, ncc=ncc),
        grid=(bsz, DN_HEADS),
        in_specs=in_specs,
        out_specs=col(n, 0),
        out_shape=jax.ShapeDtypeStruct((bsz, n, DN_WIDTH), BF16),
        scratch_shapes=scratch,
        compiler_params=pltpu.CompilerParams(dimension_semantics=("arbitrary", "arbitrary"),
                                             vmem_limit_bytes=VMEM_LIMIT),
        name="deltanet",
    )(qkv_x, qkv_x, qkv_x, aux_x, gct_x, qkv_c, qkv_c, qkv_c, aux_c, gct_c,
      conv_w, conv_w, conv_w, sg, norm_w)


_F_ROWS = 128


def _fourier_kernel(fv_ref, m_ref, r_ref, out_ref, pq_s, *, n4):
    k1 = pl.program_id(1)
    rb = min(_F_ROWS, n4)

    def combo(x, k):
        if k == 0:
            return x[0] + x[1] + x[2] + x[3], None
        if k == 2:
            return x[0] - x[1] + x[2] - x[3], None
        if k == 1:
            return x[0] - x[2], x[3] - x[1]
        return x[0] - x[2], x[1] - x[3]

    r = r_ref[...]
    for k in range(4):
        @pl.when(k1 == k)
        def _():
            def body(i, carry):
                r0 = pl.multiple_of(i * rb, rb)
                x = [fv_ref[0, pl.ds(j * n4 + r0, rb), :].astype(F32) for j in range(4)]
                d, e = combo(x, k)
                for g in range(F_GROUPS):
                    sl = slice(g * F_GROUP_DIM, (g + 1) * F_GROUP_DIM)
                    if e is None:
                        pq = jnp.dot(d[:, sl].astype(BF16), r[:F_GROUP_DIM, :], preferred_element_type=F32)
                    else:
                        de = jnp.concatenate([d[:, sl], e[:, sl]], axis=-1).astype(BF16)
                        pq = jnp.dot(de, r, preferred_element_type=F32)
                    pq_s[pl.ds(r0, rb), sl] = pq[:, :F_GROUP_DIM].astype(BF16)
                    pq_s[pl.ds(n4 + r0, rb), sl] = pq[:, F_GROUP_DIM:].astype(BF16)
                return carry
            lax.fori_loop(0, n4 // rb, body, 0)

    def rows(i, carry):
        r0 = pl.multiple_of(i * rb, rb)
        res = jnp.dot(m_ref[0, pl.ds(r0, rb), :], pq_s[...], preferred_element_type=F32)
        for g in range(F_GROUPS):
            out_ref[0, g, pl.ds(k1 + 4 * r0, rb, stride=4), :] = res[:, g * F_GROUP_DIM:(g + 1) * F_GROUP_DIM]
        return carry
    lax.fori_loop(0, n4 // rb, rows, 0)


def _fourier(fv, mcat, rmat):
    bsz, n, _ = fv.shape
    n4 = n // 4
    return pl.pallas_call(
        functools.partial(_fourier_kernel, n4=n4),
        grid=(bsz, 4),
        in_specs=[pl.BlockSpec((1, n, F_WIDTH), lambda b, k: (b, 0, 0)),
                  pl.BlockSpec((1, n4, 2 * n4), lambda b, k: (k, 0, 0)),
                  pl.BlockSpec((2 * F_GROUP_DIM, 2 * F_GROUP_DIM), lambda b, k: (0, 0))],
        out_specs=pl.BlockSpec((1, F_GROUPS, n, F_GROUP_DIM), lambda b, k: (b, 0, 0, 0)),
        out_shape=jax.ShapeDtypeStruct((bsz, F_GROUPS, n, F_GROUP_DIM), F32),
        scratch_shapes=[pltpu.VMEM((2 * n4, F_WIDTH), BF16)],
        compiler_params=pltpu.CompilerParams(dimension_semantics=("arbitrary", "arbitrary"),
                                             vmem_limit_bytes=VMEM_LIMIT),
        name="fourier",
    )(fv, mcat, rmat)


def _merge_kernel(x_ref, mod_ref, er_ref, ec_ref, mix_ref, fg_ref, og_ref, rf_ref, rd_ref,
                  wfm_ref, wfo_ref, wdo_ref, wo_ref, lng_ref, lnb_ref, out_ref, *, tm, alpha):
    t = pl.program_id(1)
    mixed = jnp.concatenate(
        [jnp.dot(mix_ref[0, g].astype(BF16), wfm_ref[g], preferred_element_type=F32)
         for g in range(F_GROUPS)], axis=-1)
    y_f = jnp.dot((mixed * fg_ref[0].astype(F32)).astype(BF16), wfo_ref[...], preferred_element_type=F32)
    y_d = jnp.dot(og_ref[0], wdo_ref[...], preferred_element_type=F32)
    m = rf_ref[0].astype(F32) * y_f + rd_ref[0].astype(F32) * y_d
    o = jnp.dot(m.astype(BF16), wo_ref[...], preferred_element_type=F32)
    gate = mod_ref[0][:, 2 * D_MODEL:3 * D_MODEL]
    x = _add_pos(x_ref[0], er_ref, ec_ref, t, tm)
    out_ref[0] = _layer_norm(alpha * x + gate * o) * lng_ref[...] + lnb_ref[...]


def _merge(x, mod3, er, ec, mixed, fg, og, rf, rd, wfm, wfo, wdo, wo, ln_g, ln_b, *, tm, alpha):
    bsz, n, _ = x.shape
    tok = lambda w: pl.BlockSpec((1, tm, w), lambda b, t: (b, t, 0))
    consts = [wfm, wfo, wdo, wo, ln_g, ln_b]
    in_specs = [tok(D_MODEL),
                pl.BlockSpec((1, 1, 3 * D_MODEL), lambda b, t: (b, 0, 0)),
                _const_spec(er.shape), _const_spec(ec.shape),
                pl.BlockSpec((1, F_GROUPS, tm, F_GROUP_DIM), lambda b, t: (b, 0, t, 0)),
                tok(F_WIDTH), tok(DN_WIDTH), tok(D_MODEL), tok(D_MODEL)]
    in_specs += [_const_spec(a.shape) for a in consts]
    return pl.pallas_call(
        functools.partial(_merge_kernel, tm=tm, alpha=alpha),
        grid=(bsz, n // tm),
        in_specs=in_specs,
        out_specs=tok(D_MODEL),
        out_shape=jax.ShapeDtypeStruct((bsz, n, D_MODEL), F32),
        compiler_params=pltpu.CompilerParams(dimension_semantics=("arbitrary", "arbitrary"),
                                             vmem_limit_bytes=VMEM_LIMIT),
        name="merge",
    )(x, mod3, er, ec, mixed, fg, og, rf, rd, *consts)


def _pos_tables(rows):
    quarter = D_MODEL // 4
    omega = 1.0 / (10000.0 ** (np.arange(quarter, dtype=np.float64) / quarter))
    pr = np.arange(rows, dtype=np.float64)[:, None] * omega
    pc = np.arange(GRID_W, dtype=np.float64)[:, None] * omega
    er = np.concatenate([np.sin(pr), np.cos(pr)], axis=-1)
    ec = np.concatenate([np.sin(pc), np.cos(pc)], axis=-1)
    return jnp.asarray(er, F32), jnp.asarray(ec, F32)


def _chunk_sum_matrices(tm):
    i = np.arange(tm)
    same = (i[:, None] // CHUNK) == (i[None, :] // CHUNK)
    lower = same & (i[:, None] >= i[None, :])
    upper = same & (i[:, None] <= i[None, :])
    return jnp.asarray(lower, BF16), jnp.asarray(upper, BF16)


def _channel_dft_matrix():
    c = np.arange(F_GROUP_DIM)
    ang = 2.0 * np.pi * ((c[:, None] * c[None, :]) % F_GROUP_DIM) / F_GROUP_DIM
    cc, sc = np.cos(ang), np.sin(ang)
    r = np.block([[cc, sc], [sc, -cc]]) / math.sqrt(F_GROUP_DIM)
    return jnp.asarray(r, F32).astype(BF16)


def _position_dft_matrices(n):
    n4 = n // 4
    k1 = jnp.arange(4, dtype=jnp.int32)[:, None, None]
    k2 = jnp.arange(n4, dtype=jnp.int32)[None, :, None]
    n2 = jnp.arange(n4, dtype=jnp.int32)[None, None, :]
    ang = ((n2 * (k1 + 4 * k2)) % n).astype(F32) * (2.0 * math.pi / n)
    scale = 1.0 / math.sqrt(n)
    return jnp.concatenate([jnp.cos(ang) * scale, -jnp.sin(ang) * scale], axis=-1).astype(BF16)


def _aux_weights(w_beta, w_decay, a_log, dt_bias):
    reps = LANES // AUX_GROUP
    wab = jnp.tile(w_beta, (1, reps)).astype(BF16)
    wad = jnp.tile(w_decay, (1, reps)).astype(BF16)
    wdt = w_decay.T.astype(BF16)
    al = a_log.reshape(1, AUX_GROUP).astype(F32)
    db = dt_bias.reshape(1, AUX_GROUP).astype(F32)
    lvec = jnp.concatenate([jnp.tile(al, (1, reps)), jnp.tile(db, (1, reps)),
                            jnp.zeros((6, LANES), F32)], axis=0)
    cvec = jnp.concatenate([al.T, db.T], axis=1)
    return wab, wad, wdt, lvec, cvec


def kernel(x, c, ctx, c_ctx, w_mod, b_mod, w_in, conv_w, a_log, dt_bias, dn_norm_w, w_dn_out, w_fmix,
           w_f_out, w_out, ln_g, ln_b):
    depth = w_mod.shape[0]
    assert depth == 1, "single-layer configuration"
    bsz, n, _ = x.shape
    nctx = ctx.shape[1]
    assert n % (4 * CHUNK) == 0 and nctx % CHUNK == 0 and bsz <= 8
    alpha = (2 * depth) ** 0.25
    tm = min(512, n)
    tmc = min(256, nctx)

    cc = jnp.zeros((16, D_MODEL), F32).at[:bsz].set(c).at[8].set(c_ctx)
    mod3 = _modulation(cc, w_mod[0], b_mod[0].reshape(1, -1)).reshape(16, 1, 3 * D_MODEL)

    w = w_in[0]
    o0, o1, o2, o3, o4, o5, o6 = (int(v) for v in np.cumsum(
        (F_WIDTH, F_WIDTH, 3 * DN_WIDTH, DN_WIDTH, AUX_GROUP, AUX_GROUP, D_MODEL)))
    wf = w[:, :o1].astype(BF16)
    wqkv = w[:, o1:o2].astype(BF16)
    wg = w[:, o2:o3].astype(BF16)
    wr = w[:, o5:].astype(BF16)
    wab, wad, wdt, lvec, cvec = _aux_weights(w[:, o3:o4], w[:, o4:o5], a_log[0], dt_bias[0])
    er, ec = _pos_tables(n // GRID_W)

    lblk, ublk = _chunk_sum_matrices(tm)
    fv, fg, qkv_x, sg, rf, rd, aux_x, gct_x = _inproj(
        x, mod3, lambda b: b, (er, ec), (wf, wqkv, wg, wr, wab, wad, wdt), (lvec, cvec, lblk, ublk),
        tm=tm, latent=True)
    lblk_c, ublk_c = _chunk_sum_matrices(tmc)
    qkv_c, aux_c, gct_c = _inproj(
        ctx, mod3, lambda b: 8, None, (wf, wqkv, wg, wr, wab, wad, wdt), (lvec, cvec, lblk_c, ublk_c),
        tm=tmc, latent=False)

    og = _deltanet(qkv_x, aux_x, gct_x, qkv_c, aux_c, gct_c, conv_w[0], sg,
                   dn_norm_w[0].reshape(1, -1).astype(F32))
    mixed = _fourier(fv, _position_dft_matrices(n), _channel_dft_matrix())
    return _merge(x, mod3, er, ec, mixed, fg, og, rf, rd,
                  w_fmix[0].astype(BF16), w_f_out[0].astype(BF16), w_dn_out[0].astype(BF16),
                  w_out[0].astype(BF16), ln_g[0].reshape(1, -1), ln_b[0].reshape(1, -1),
                  tm=tm, alpha=alpha)
```

```python
import functools
import math

import numpy as np
import jax
import jax.numpy as jnp
from jax import lax
from jax.experimental import pallas as pl
from jax.experimental.pallas import tpu as pltpu

F32 = jnp.float32
BF16 = jnp.bfloat16

D_MODEL = 1024
GRID_W = 64
F_GROUPS = 4
F_GROUP_DIM = 128
F_WIDTH = F_GROUPS * F_GROUP_DIM
DN_HEADS = 8
DN_HEAD_DIM = 128
DN_WIDTH = DN_HEADS * DN_HEAD_DIM
N_DIR = 2
CHUNK = 64
EPS = 1e-6
LANES = 128
AUX_GROUP = 16
VMEM_LIMIT = 56 * 1024 * 1024


def _silu(x):
    return x * jax.nn.sigmoid(x)


def _softplus(x):
    return jnp.maximum(x, 0.0) + jnp.log1p(jnp.exp(-jnp.abs(x)))


def _dot_nt(a, b):
    return lax.dot_general(a.astype(BF16), b.astype(BF16), (((1,), (1,)), ((), ())),
                           preferred_element_type=F32)


def _split3(x):
    hi = x.astype(BF16)
    r1 = x - hi.astype(F32)
    mid = r1.astype(BF16)
    lo = (r1 - mid.astype(F32)).astype(BF16)
    return hi, mid, lo


def _layer_norm(x):
    mu = jnp.mean(x, axis=-1, keepdims=True)
    xc = x - mu
    var = jnp.mean(xc * xc, axis=-1, keepdims=True)
    return xc * lax.rsqrt(var + EPS)


def _add_pos(x, er_ref, ec_ref, t, tm):
    rows = tm // GRID_W
    half = D_MODEL // 2
    er = er_ref[pl.ds(t * rows, rows), :]
    ec = ec_ref[...]
    x3 = x.reshape(rows, GRID_W, D_MODEL)
    pe = jnp.concatenate([jnp.broadcast_to(er[:, None, :], (rows, GRID_W, half)),
                          jnp.broadcast_to(ec[None, :, :], (rows, GRID_W, half))], axis=-1)
    return (x3 + pe).reshape(tm, D_MODEL)


def _mod_kernel(c_ref, w_ref, b_ref, o_ref):
    s = _silu(c_ref[...])
    o_ref[...] = jnp.dot(s, w_ref[...], preferred_element_type=F32,
                         precision=lax.Precision.HIGHEST) + b_ref[...]


def _modulation(cc, w_mod, b_mod):
    rows = cc.shape[0]
    nblk = 3
    return pl.pallas_call(
        _mod_kernel,
        grid=(nblk,),
        in_specs=[pl.BlockSpec((rows, D_MODEL), lambda j: (0, 0)),
                  pl.BlockSpec((D_MODEL, D_MODEL), lambda j: (0, j)),
                  pl.BlockSpec((1, D_MODEL), lambda j: (0, j))],
        out_specs=pl.BlockSpec((rows, D_MODEL), lambda j: (0, j)),
        out_shape=jax.ShapeDtypeStruct((rows, 3 * D_MODEL), F32),
        compiler_params=pltpu.CompilerParams(dimension_semantics=("arbitrary",),
                                             vmem_limit_bytes=VMEM_LIMIT),
        name="mod",
    )(cc, w_mod, b_mod)


def _aux_from_raw(raw_b, raw_d, alog_l, dtb_l, lblk, tm):
    beta = jax.nn.sigmoid(raw_b)
    g = -jnp.exp(alog_l) * _softplus(raw_d + dtb_l)
    hi, mid, lo = _split3(g)
    pre = (jnp.dot(lblk, hi, preferred_element_type=F32)
           + jnp.dot(lblk, mid, preferred_element_type=F32)
           + jnp.dot(lblk, lo, preferred_element_type=F32))
    nck = tm // CHUNK
    pre3 = pre.reshape(nck, CHUNK, LANES)
    tot = jnp.broadcast_to(pre3[:, CHUNK - 1:CHUNK, :], (nck, CHUNK, LANES)).reshape(tm, LANES)
    suf = tot - pre + g
    lane = lax.broadcasted_iota(jnp.int32, (tm, LANES), 1)
    backward = (lane % AUX_GROUP) >= DN_HEADS
    gc = jnp.where(backward, suf, pre)
    egc = jnp.exp(gc)
    grp = lane // AUX_GROUP
    aux = jnp.where(grp == 0, beta,
          jnp.where(grp == 1, gc,
          jnp.where(grp == 2, egc,
          jnp.where(grp == 3, beta * egc,
          jnp.where(grp == 4, jnp.exp(tot - gc),
          jnp.where(grp == 5, jnp.exp(tot), 0.0))))))
    return aux


def _gct_from_raw(raw_t, alog_c, dtb_c, lblk, ublk):
    g = -jnp.exp(alog_c) * _softplus(raw_t + dtb_c)
    hi, mid, lo = _split3(g)
    pre = (jnp.dot(hi, ublk, preferred_element_type=F32)
           + jnp.dot(mid, ublk, preferred_element_type=F32)
           + jnp.dot(lo, ublk, preferred_element_type=F32))
    suf = (jnp.dot(hi, lblk, preferred_element_type=F32)
           + jnp.dot(mid, lblk, preferred_element_type=F32)
           + jnp.dot(lo, lblk, preferred_element_type=F32))
    row = lax.broadcasted_iota(jnp.int32, g.shape, 0)
    return jnp.where(row >= DN_HEADS, suf, pre)


_HALO = 8
_QKV_BLOCK = 256


def _qkv_conv(h_ext, wqkv_ref, cw_ref, qkv_ref, tm):
    rows = tm + 2 * _HALO
    for j in range(3 * DN_WIDTH // _QKV_BLOCK):
        cols = slice(j * _QKV_BLOCK, (j + 1) * _QKV_BLOCK)
        p = jnp.dot(h_ext, wqkv_ref[:, cols], preferred_element_type=F32)
        w = cw_ref[:, cols]
        y = (pltpu.roll(p, 1, axis=0)[_HALO:_HALO + tm] * w[0:1, :] + p[_HALO:_HALO + tm] * w[1:2, :]
             + pltpu.roll(p, rows - 1, axis=0)[_HALO:_HALO + tm] * w[2:3, :])
        y = _silu(y)
        if j * _QKV_BLOCK < 2 * DN_WIDTH:
            gain = DN_HEAD_DIM ** -0.5 if j * _QKV_BLOCK < DN_WIDTH else 1.0
            heads = [y[:, i * DN_HEAD_DIM:(i + 1) * DN_HEAD_DIM] for i in range(_QKV_BLOCK // DN_HEAD_DIM)]
            y = jnp.concatenate([yh * (lax.rsqrt(jnp.sum(yh * yh, axis=-1, keepdims=True) + EPS) * gain)
                                 for yh in heads], axis=-1)
        qkv_ref[0, :, cols] = y.astype(BF16)


def _inproj_kernel(*refs, tm, latent):
    if latent:
        (x_ref, xp_ref, xn_ref, mod_ref, er_ref, ec_ref, wf_ref, wqkv_ref, wg_ref, wr_ref, wab_ref, wad_ref,
         wdt_ref, cw_ref, lvec_ref, cvec_ref, lblk_ref, ublk_ref,
         fv_ref, fg_ref, qkv_ref, sg_ref, rf_ref, rd_ref, aux_ref, gct_ref) = refs
    else:
        (x_ref, xp_ref, xn_ref, mod_ref, wqkv_ref, wab_ref, wad_ref, wdt_ref, cw_ref,
         lvec_ref, cvec_ref, lblk_ref, ublk_ref,
         qkv_ref, aux_ref, gct_ref) = refs
    t = pl.program_id(1)
    nt = pl.num_programs(1)
    x = x_ref[0]
    xp = xp_ref[0]
    xn = xn_ref[0]
    if latent:
        x = _add_pos(x, er_ref, ec_ref, t, tm)
        rows = tm // GRID_W
        last = er_ref.shape[0] - 1
        erp = er_ref[pl.ds(jnp.maximum(t * rows - 1, 0), 1), :]
        ern = er_ref[pl.ds(jnp.minimum((t + 1) * rows, last), 1), :]
        xp = xp + jnp.concatenate([jnp.broadcast_to(erp, (_HALO, D_MODEL // 2)),
                                   ec_ref[GRID_W - _HALO:GRID_W, :]], axis=-1)
        xn = xn + jnp.concatenate([jnp.broadcast_to(ern, (_HALO, D_MODEL // 2)), ec_ref[0:_HALO, :]], axis=-1)
    mod = mod_ref[0]
    shift = mod[:, 0:D_MODEL]
    scale = mod[:, D_MODEL:2 * D_MODEL]
    hf = _layer_norm(x) * (1.0 + scale) + shift
    h = hf.astype(BF16)
    hp = (_layer_norm(xp) * (1.0 + scale) + shift) * jnp.where(t > 0, 1.0, 0.0)
    hn = (_layer_norm(xn) * (1.0 + scale) + shift) * jnp.where(t < nt - 1, 1.0, 0.0)
    _qkv_conv(jnp.concatenate([hp, hf, hn], axis=0).astype(BF16), wqkv_ref, cw_ref, qkv_ref, tm)
    if latent:
        pf = jnp.dot(h, wf_ref[...], preferred_element_type=F32)
        fv_ref[0] = pf[:, :F_WIDTH].astype(BF16)
        fg_ref[0] = _silu(pf[:, F_WIDTH:]).astype(BF16)
        sg_ref[0] = _silu(jnp.dot(h, wg_ref[...], preferred_element_type=F32)).astype(BF16)
        pr = jnp.dot(h, wr_ref[...], preferred_element_type=F32)
        rf_ref[0] = jax.nn.sigmoid(pr[:, :D_MODEL]).astype(BF16)
        rd_ref[0] = jax.nn.sigmoid(pr[:, D_MODEL:]).astype(BF16)

    lblk = lblk_ref[...]
    raw_b = jnp.dot(h, wab_ref[...], preferred_element_type=F32)
    raw_d = jnp.dot(h, wad_ref[...], preferred_element_type=F32)
    aux_ref[0] = _aux_from_raw(raw_b, raw_d, lvec_ref[0:1, :], lvec_ref[1:2, :], lblk, tm)
    raw_t = lax.dot_general(wdt_ref[...], h, (((1,), (1,)), ((), ())), preferred_element_type=F32)
    gct_ref[0] = _gct_from_raw(raw_t, cvec_ref[:, 0:1], cvec_ref[:, 1:2], lblk, ublk_ref[...])


def _const_spec(shape):
    nd = len(shape)
    return pl.BlockSpec(shape, lambda b, t: (0,) * nd)


def _inproj(xin, mod3, mod_row_fn, tables, weights, consts, *, tm, latent):
    bsz, n, _ = xin.shape
    nt = n // tm
    wf, wqkv, wg, wr, wab, wad, wdt, cw = weights
    lvec, cvec, lblk, ublk = consts
    tok = lambda w: pl.BlockSpec((1, tm, w), lambda b, t: (b, t, 0))
    x_spec = pl.BlockSpec((1, tm, D_MODEL), lambda b, t: (b, t, 0))
    per = tm // _HALO
    xp_spec = pl.BlockSpec((1, _HALO, D_MODEL), lambda b, t: (b, jnp.maximum(t * per - 1, 0), 0))
    xn_spec = pl.BlockSpec((1, _HALO, D_MODEL), lambda b, t: (b, jnp.minimum((t + 1) * per, n // _HALO - 1), 0))
    mod_spec = pl.BlockSpec((1, 1, 3 * D_MODEL), lambda b, t: (mod_row_fn(b), 0, 0))
    gct_spec = pl.BlockSpec((1, AUX_GROUP, tm), lambda b, t: (b, 0, t))
    sds = lambda w, dt: jax.ShapeDtypeStruct((bsz, n, w), dt)
    gct_sds = jax.ShapeDtypeStruct((bsz, AUX_GROUP, n), F32)
    tail_in = [lvec, cvec, lblk, ublk]
    tail_specs = [_const_spec(a.shape) for a in tail_in]
    if latent:
        er, ec = tables
        ins = [xin, xin, xin, mod3, er, ec, wf, wqkv, wg, wr, wab, wad, wdt, cw] + tail_in
        in_specs = ([x_spec, xp_spec, xn_spec, mod_spec] + [_const_spec(a.shape) for a in ins[4:14]] + tail_specs)
        out_specs = [tok(F_WIDTH), tok(F_WIDTH), tok(3 * DN_WIDTH), tok(DN_WIDTH), tok(D_MODEL),
                     tok(D_MODEL), tok(LANES), gct_spec]
        out_shape = [sds(F_WIDTH, BF16), sds(F_WIDTH, BF16), sds(3 * DN_WIDTH, BF16), sds(DN_WIDTH, BF16),
                     sds(D_MODEL, BF16), sds(D_MODEL, BF16), sds(LANES, F32), gct_sds]
    else:
        ins = [xin, xin, xin, mod3, wqkv, wab, wad, wdt, cw] + tail_in
        in_specs = ([x_spec, xp_spec, xn_spec, mod_spec] + [_const_spec(a.shape) for a in ins[4:9]] + tail_specs)
        out_specs = [tok(3 * DN_WIDTH), tok(LANES), gct_spec]
        out_shape = [sds(3 * DN_WIDTH, BF16), sds(LANES, F32), gct_sds]
    return pl.pallas_call(
        functools.partial(_inproj_kernel, tm=tm, latent=latent),
        grid=(bsz, nt),
        in_specs=in_specs,
        out_specs=out_specs,
        out_shape=out_shape,
        compiler_params=pltpu.CompilerParams(dimension_semantics=("arbitrary", "arbitrary"),
                                             vmem_limit_bytes=VMEM_LIMIT),
        name="inproj_latent" if latent else "inproj_ctx",
    )(*ins)


def _block_diag2(m):
    lane = lax.broadcasted_iota(jnp.int32, m.shape, 1)
    z = jnp.zeros_like(m)
    return jnp.concatenate([jnp.where(lane < CHUNK, m, z), jnp.where(lane >= CHUNK, m, z)], axis=0).astype(BF16)


def _dn_masks():
    ri = lax.broadcasted_iota(jnp.int32, (CHUNK, 2 * CHUNK), 0)
    lane = lax.broadcasted_iota(jnp.int32, (CHUNK, 2 * CHUNK), 1)
    fwd = lane < CHUNK
    ci = jnp.where(fwd, lane, lane - CHUNK)
    bwd = jnp.logical_not(fwd)
    incl = (fwd & (ri >= ci)) | (bwd & (ri <= ci))
    strict = (fwd & (ri > ci)) | (bwd & (ri < ci))
    same = lambda k: (ri >> k) == (ci >> k)
    levels = [same(k + 1) & jnp.logical_not(same(k)) for k in range(1, int(math.log2(CHUNK)))]
    eye = jnp.where(ri == ci, 1.0, 0.0)
    return fwd, incl, strict, same(1), levels, eye


def _dn_side(refs, c, h):
    q_ref, k_ref, v_ref, aux_ref = refs
    rows = pl.ds(pl.multiple_of(c * CHUNK, CHUNK), CHUNK)
    aux = pltpu.roll(aux_ref[0, rows, :], (LANES - h) % LANES, axis=1)
    return q_ref[0, rows, :].astype(F32), k_ref[0, rows, :].astype(F32), v_ref[0, rows, :].astype(F32), aux


def _dn_prepare(pairs, h, scr, between=()):
    mq_s, b_s, egt_s = scr
    fwd, incl, strict, same1, levels, eye = _dn_masks()
    between = list(between)
    n_slots = 2 * len(levels) + 2
    per_slot = -(-len(between) // n_slots)

    def slot():
        for _ in range(per_slot):
            if between:
                between.pop(0)()

    col = lambda aux, j, d: aux[:, j * AUX_GROUP + d * DN_HEADS:j * AUX_GROUP + d * DN_HEADS + 1]
    zk = jnp.zeros((CHUNK, DN_HEAD_DIM), F32)
    sides, ls, decays, qks = [], [], [], []
    for (refs_f, cf, refs_b, cb, gt_ref, gt_row, _, _) in pairs:
        qa, ka, va, auxa = _dn_side(refs_f, cf, h)
        qb, kb, vb, auxb = _dn_side(refs_b, cb, h)
        gram = (_dot_nt(jnp.concatenate([ka, qa], axis=0), jnp.concatenate([ka, zk], axis=0))
                + _dot_nt(jnp.concatenate([kb, qb], axis=0), jnp.concatenate([zk, kb], axis=0)))
        kk, qk = gram[:CHUNK], gram[CHUNK:]
        both = lambda j: jnp.where(fwd, col(auxa, j, 0), col(auxb, j, 1))
        gr = gt_ref[0, h, pl.ds(gt_row, 1), :]
        decay = jnp.exp(jnp.where(incl, both(1) - gr, -jnp.inf))
        ls.append(jnp.where(strict, kk * decay * both(0), 0.0))
        sides.append((qa, ka, va, auxa, qb, kb, vb, auxb)); decays.append(decay); qks.append(qk)
    slot()

    xs = [eye - jnp.where(same1, l, 0.0) for l in ls]
    for lvl in levels:
        ts = [jnp.dot(x.astype(BF16), _block_diag2(jnp.where(lvl, l, 0.0)), preferred_element_type=F32)
              for x, l in zip(xs, ls)]
        slot()
        xs = [x - jnp.dot(t.astype(BF16), _block_diag2(x), preferred_element_type=F32) for x, t in zip(xs, ts)]
        slot()

    z = jnp.zeros((CHUNK, 2 * DN_HEAD_DIM), BF16)
    sols = []
    for x, (qa, ka, va, auxa, qb, kb, vb, auxb) in zip(xs, sides):
        rhs_f = jnp.concatenate([va * col(auxa, 0, 0), ka * col(auxa, 3, 0)], axis=-1).astype(BF16)
        rhs_b = jnp.concatenate([vb * col(auxb, 0, 1), kb * col(auxb, 3, 1)], axis=-1).astype(BF16)
        rhs = jnp.concatenate([jnp.concatenate([rhs_f, z], axis=-1), jnp.concatenate([z, rhs_b], axis=-1)], axis=0)
        sols.append(jnp.dot(x.astype(BF16), rhs, preferred_element_type=F32).astype(BF16))
    slot()
    mbs = []
    for sol, (qa, ka, va, auxa, qb, kb, vb, auxb), qk, decay in zip(sols, sides, qks, decays):
        uw = jnp.concatenate([jnp.concatenate([sol[:, :2 * DN_HEAD_DIM], z], axis=-1),
                              jnp.concatenate([z, sol[:, 2 * DN_HEAD_DIM:]], axis=-1)], axis=0)
        kdt = jnp.concatenate([ka * col(auxa, 4, 0), kb * col(auxb, 4, 1)], axis=0).T
        lhs = jnp.concatenate([kdt, qk * decay], axis=0).astype(BF16)
        mbs.append(jnp.dot(lhs, uw, preferred_element_type=F32))
    while between:
        between.pop(0)()
    w = DN_HEAD_DIM
    for (refs_f, cf, refs_b, cb, gt_ref, gt_row, pos, o0), mb, side in zip(pairs, mbs, sides):
        for d, (q, aux, c) in enumerate(((side[0], side[3], cf), (side[4], side[7], cb))):
            b_s[d, pos] = mb[:w, (2 * d) * w:(2 * d + 1) * w]
            mq_s[d, pos, 0:w, :] = mb[:w, (2 * d + 1) * w:(2 * d + 2) * w].astype(BF16)
            egt_s[d, pos] = jnp.broadcast_to(col(aux, 5, d)[0:1, :], (8, LANES))
            if o0 is not None:
                mq_s[d, pos, w:w + CHUNK, :] = (q * col(aux, 2, d)
                                                 - mb[w:, (2 * d + 1) * w:(2 * d + 2) * w]).astype(BF16)
                o0[d][pl.ds(pl.multiple_of(c * CHUNK, CHUNK), CHUNK), :] = mb[w:, (2 * d) * w:(2 * d + 1) * w]


def _dn_step(d, pos, scr, s_ref, o_ref=None, chunk=None):
    mq_s, b_s, egt_s = scr
    w = DN_HEAD_DIM
    s = s_ref[d]
    sb = s.astype(BF16)
    egt = egt_s[d, pos]
    s3 = (s.reshape(w // 8, 8, LANES) * egt[None]).reshape(w, LANES)
    if o_ref is None:
        ms = jnp.dot(mq_s[d, pos, 0:w, :], sb, preferred_element_type=F32)
    else:
        mqs = jnp.dot(mq_s[d, pos], sb, preferred_element_type=F32)
        ms = mqs[:w]
        rows = pl.ds(pl.multiple_of(chunk * CHUNK, CHUNK), CHUNK)
        o_ref[rows, :] = o_ref[rows, :] + mqs[w:]
    s_ref[d] = s3 - ms + b_s[d, pos]


_DN_GROUP = 8
_DN_FINISH_ROWS = 256


def _dn_kernel(qx, kx, vx, auxx, gtx, qc, kc, vc, auxc, gtc, sg_ref, nw_ref, out_ref,
               mq_s, b_s, egt_s, of_s, ob_s, s_ref, *, ncx, ncc):
    h = pl.program_id(1)
    scr = (mq_s, b_s, egt_s)
    lat = (qx, kx, vx, auxx)
    ctx = (qc, kc, vc, auxc)
    ngroups = ncx // _DN_GROUP

    def ctx_pair(p):
        return (ctx, p, ctx, ncc - 1 - p, gtc, p, p, None)

    def lat_pair(j):
        return (lat, j, lat, ncx - 1 - j, gtx, j, ncc + j, (of_s, ob_s))

    def ctx_step(p):
        _dn_step(0, p, scr, s_ref)
        _dn_step(1, p, scr, s_ref)

    def lat_step(j):
        _dn_step(0, ncc + j, scr, s_ref, of_s, j)
        _dn_step(1, ncc + j, scr, s_ref, ob_s, ncx - 1 - j)

    def lat_group(g):
        return [lat_pair(g * _DN_GROUP + i) for i in range(_DN_GROUP)]

    def lat_steps(g):
        return [functools.partial(lat_step, g * _DN_GROUP + i) for i in range(_DN_GROUP)]

    _dn_prepare([ctx_pair(p) for p in range(ncc)] + lat_group(0), h, scr)
    s_ref[...] = jnp.zeros_like(s_ref)
    if ngroups > 1:
        _dn_prepare(lat_group(1), h, scr, [functools.partial(ctx_step, p) for p in range(ncc)] + lat_steps(0))

        def body(g, carry):
            _dn_prepare(lat_group(g), h, scr, lat_steps(g - 1))
            return carry

        lax.fori_loop(2, ngroups, body, 0)
    else:
        for p in range(ncc):
            ctx_step(p)
    for step in lat_steps(ngroups - 1):
        step()

    nw = nw_ref[...]
    fr = min(_DN_FINISH_ROWS, ncx * CHUNK)

    def finish(i, carry):
        rows = pl.ds(pl.multiple_of(i * fr, fr), fr)
        o = of_s[rows, :] + ob_s[rows, :]
        o = o * lax.rsqrt(jnp.mean(o * o, axis=-1, keepdims=True) + EPS) * nw
        out_ref[0, rows, :] = (o * sg_ref[0, rows, :].astype(F32)).astype(BF16)
        return carry

    lax.fori_loop(0, ncx * CHUNK // fr, finish, 0)


def _deltanet(qkv_x, aux_x, gct_x, qkv_c, aux_c, gct_c, sg, norm_w):
    bsz, n, _ = qkv_x.shape
    nctx = qkv_c.shape[1]
    ncx, ncc = n // CHUNK, nctx // CHUNK
    nct = ncx + ncc
    assert ncx % _DN_GROUP == 0

    def rows_by_position(g, nc):
        g = g.reshape(bsz, N_DIR, DN_HEADS, nc, CHUNK)
        g = jnp.stack([g[:, 0], jnp.flip(g[:, 1], axis=2)], axis=3)
        return g.reshape(bsz, DN_HEADS, nc, N_DIR * CHUNK)

    gct_x = rows_by_position(gct_x, ncx)
    gct_c = rows_by_position(gct_c, ncc)
    col = lambda rows, off: pl.BlockSpec((1, rows, DN_HEAD_DIM), lambda b, h: (b, 0, off + h))
    whole = lambda a: pl.BlockSpec((1,) + a.shape[1:], lambda b, h: (b,) + (0,) * (a.ndim - 1))
    in_specs = [col(n, 0), col(n, DN_HEADS), col(n, 2 * DN_HEADS), whole(aux_x), whole(gct_x),
                col(nctx, 0), col(nctx, DN_HEADS), col(nctx, 2 * DN_HEADS), whole(aux_c), whole(gct_c),
                col(n, 0), pl.BlockSpec((1, DN_HEAD_DIM), lambda b, h: (0, 0))]
    scratch = [pltpu.VMEM((N_DIR, nct, DN_HEAD_DIM + CHUNK, DN_HEAD_DIM), BF16),
               pltpu.VMEM((N_DIR, nct, DN_HEAD_DIM, DN_HEAD_DIM), F32),
               pltpu.VMEM((N_DIR, nct, 8, LANES), F32),
               pltpu.VMEM((n, DN_HEAD_DIM), F32),
               pltpu.VMEM((n, DN_HEAD_DIM), F32),
               pltpu.VMEM((N_DIR, DN_HEAD_DIM, DN_HEAD_DIM), F32)]
    return pl.pallas_call(
        functools.partial(_dn_kernel, ncx=ncx, ncc=ncc),
        grid=(bsz, DN_HEADS),
        in_specs=in_specs,
        out_specs=col(n, 0),
        out_shape=jax.ShapeDtypeStruct((bsz, n, DN_WIDTH), BF16),
        scratch_shapes=scratch,
        compiler_params=pltpu.CompilerParams(dimension_semantics=("arbitrary", "arbitrary"),
                                             vmem_limit_bytes=VMEM_LIMIT),
        name="deltanet",
    )(qkv_x, qkv_x, qkv_x, aux_x, gct_x, qkv_c, qkv_c, qkv_c, aux_c, gct_c, sg, norm_w)


_F_ROWS = 128


def _fourier_kernel(fv_ref, m_ref, r_ref, out_ref, pq_s, *, n4):
    k1 = pl.program_id(1)
    rb = min(_F_ROWS, n4)

    def combo(x, k):
        if k == 0:
            return x[0] + x[1] + x[2] + x[3], None
        if k == 2:
            return x[0] - x[1] + x[2] - x[3], None
        if k == 1:
            return x[0] - x[2], x[3] - x[1]
        return x[0] - x[2], x[1] - x[3]

    r = r_ref[...]
    for k in range(4):
        @pl.when(k1 == k)
        def _():
            def body(i, carry):
                r0 = pl.multiple_of(i * rb, rb)
                x = [fv_ref[0, pl.ds(j * n4 + r0, rb), :].astype(F32) for j in range(4)]
                d, e = combo(x, k)
                for g in range(F_GROUPS):
                    sl = slice(g * F_GROUP_DIM, (g + 1) * F_GROUP_DIM)
                    if e is None:
                        pq = jnp.dot(d[:, sl].astype(BF16), r[:F_GROUP_DIM, :], preferred_element_type=F32)
                    else:
                        de = jnp.concatenate([d[:, sl], e[:, sl]], axis=-1).astype(BF16)
                        pq = jnp.dot(de, r, preferred_element_type=F32)
                    pq_s[pl.ds(r0, rb), sl] = pq[:, :F_GROUP_DIM].astype(BF16)
                    pq_s[pl.ds(n4 + r0, rb), sl] = pq[:, F_GROUP_DIM:].astype(BF16)
                return carry
            lax.fori_loop(0, n4 // rb, body, 0)

    def rows(i, carry):
        r0 = pl.multiple_of(i * rb, rb)
        res = jnp.dot(m_ref[0, pl.ds(r0, rb), :], pq_s[...], preferred_element_type=F32)
        for g in range(F_GROUPS):
            out_ref[0, g, pl.ds(k1 + 4 * r0, rb, stride=4), :] = res[:, g * F_GROUP_DIM:(g + 1) * F_GROUP_DIM]
        return carry
    lax.fori_loop(0, n4 // rb, rows, 0)


def _fourier(fv, mcat, rmat):
    bsz, n, _ = fv.shape
    n4 = n // 4
    return pl.pallas_call(
        functools.partial(_fourier_kernel, n4=n4),
        grid=(bsz, 4),
        in_specs=[pl.BlockSpec((1, n, F_WIDTH), lambda b, k: (b, 0, 0)),
                  pl.BlockSpec((1, n4, 2 * n4), lambda b, k: (k, 0, 0)),
                  pl.BlockSpec((2 * F_GROUP_DIM, 2 * F_GROUP_DIM), lambda b, k: (0, 0))],
        out_specs=pl.BlockSpec((1, F_GROUPS, n, F_GROUP_DIM), lambda b, k: (b, 0, 0, 0)),
        out_shape=jax.ShapeDtypeStruct((bsz, F_GROUPS, n, F_GROUP_DIM), F32),
        scratch_shapes=[pltpu.VMEM((2 * n4, F_WIDTH), BF16)],
        compiler_params=pltpu.CompilerParams(dimension_semantics=("arbitrary", "arbitrary"),
                                             vmem_limit_bytes=VMEM_LIMIT),
        name="fourier",
    )(fv, mcat, rmat)


def _merge_kernel(x_ref, mod_ref, er_ref, ec_ref, mix_ref, fg_ref, og_ref, rf_ref, rd_ref,
                  wfm_ref, wfo_ref, wdo_ref, wo_ref, lng_ref, lnb_ref, out_ref, *, tm, alpha):
    t = pl.program_id(1)
    mixed = jnp.concatenate(
        [jnp.dot(mix_ref[0, g].astype(BF16), wfm_ref[g], preferred_element_type=F32)
         for g in range(F_GROUPS)], axis=-1)
    y_f = jnp.dot((mixed * fg_ref[0].astype(F32)).astype(BF16), wfo_ref[...], preferred_element_type=F32)
    y_d = jnp.dot(og_ref[0], wdo_ref[...], preferred_element_type=F32)
    m = rf_ref[0].astype(F32) * y_f + rd_ref[0].astype(F32) * y_d
    o = jnp.dot(m.astype(BF16), wo_ref[...], preferred_element_type=F32)
    gate = mod_ref[0][:, 2 * D_MODEL:3 * D_MODEL]
    x = _add_pos(x_ref[0], er_ref, ec_ref, t, tm)
    out_ref[0] = _layer_norm(alpha * x + gate * o) * lng_ref[...] + lnb_ref[...]


def _merge(x, mod3, er, ec, mixed, fg, og, rf, rd, wfm, wfo, wdo, wo, ln_g, ln_b, *, tm, alpha):
    bsz, n, _ = x.shape
    tok = lambda w: pl.BlockSpec((1, tm, w), lambda b, t: (b, t, 0))
    consts = [wfm, wfo, wdo, wo, ln_g, ln_b]
    in_specs = [tok(D_MODEL),
                pl.BlockSpec((1, 1, 3 * D_MODEL), lambda b, t: (b, 0, 0)),
                _const_spec(er.shape), _const_spec(ec.shape),
                pl.BlockSpec((1, F_GROUPS, tm, F_GROUP_DIM), lambda b, t: (b, 0, t, 0)),
                tok(F_WIDTH), tok(DN_WIDTH), tok(D_MODEL), tok(D_MODEL)]
    in_specs += [_const_spec(a.shape) for a in consts]
    return pl.pallas_call(
        functools.partial(_merge_kernel, tm=tm, alpha=alpha),
        grid=(bsz, n // tm),
        in_specs=in_specs,
        out_specs=tok(D_MODEL),
        out_shape=jax.ShapeDtypeStruct((bsz, n, D_MODEL), F32),
        compiler_params=pltpu.CompilerParams(dimension_semantics=("arbitrary", "arbitrary"),
                                             vmem_limit_bytes=VMEM_LIMIT),
        name="merge",
    )(x, mod3, er, ec, mixed, fg, og, rf, rd, *consts)


def _pos_tables(rows):
    quarter = D_MODEL // 4
    omega = 1.0 / (10000.0 ** (np.arange(quarter, dtype=np.float64) / quarter))
    pr = np.arange(rows, dtype=np.float64)[:, None] * omega
    pc = np.arange(GRID_W, dtype=np.float64)[:, None] * omega
    er = np.concatenate([np.sin(pr), np.cos(pr)], axis=-1)
    ec = np.concatenate([np.sin(pc), np.cos(pc)], axis=-1)
    return jnp.asarray(er, F32), jnp.asarray(ec, F32)


def _chunk_sum_matrices(tm):
    i = np.arange(tm)
    same = (i[:, None] // CHUNK) == (i[None, :] // CHUNK)
    lower = same & (i[:, None] >= i[None, :])
    upper = same & (i[:, None] <= i[None, :])
    return jnp.asarray(lower, BF16), jnp.asarray(upper, BF16)


def _channel_dft_matrix():
    c = np.arange(F_GROUP_DIM)
    ang = 2.0 * np.pi * ((c[:, None] * c[None, :]) % F_GROUP_DIM) / F_GROUP_DIM
    cc, sc = np.cos(ang), np.sin(ang)
    r = np.block([[cc, sc], [sc, -cc]]) / math.sqrt(F_GROUP_DIM)
    return jnp.asarray(r, F32).astype(BF16)


def _position_dft_matrices(n):
    n4 = n // 4
    k1 = jnp.arange(4, dtype=jnp.int32)[:, None, None]
    k2 = jnp.arange(n4, dtype=jnp.int32)[None, :, None]
    n2 = jnp.arange(n4, dtype=jnp.int32)[None, None, :]
    ang = ((n2 * (k1 + 4 * k2)) % n).astype(F32) * (2.0 * math.pi / n)
    scale = 1.0 / math.sqrt(n)
    return jnp.concatenate([jnp.cos(ang) * scale, -jnp.sin(ang) * scale], axis=-1).astype(BF16)


def _aux_weights(w_beta, w_decay, a_log, dt_bias):
    reps = LANES // AUX_GROUP
    wab = jnp.tile(w_beta, (1, reps)).astype(BF16)
    wad = jnp.tile(w_decay, (1, reps)).astype(BF16)
    wdt = w_decay.T.astype(BF16)
    al = a_log.reshape(1, AUX_GROUP).astype(F32)
    db = dt_bias.reshape(1, AUX_GROUP).astype(F32)
    lvec = jnp.concatenate([jnp.tile(al, (1, reps)), jnp.tile(db, (1, reps)),
                            jnp.zeros((6, LANES), F32)], axis=0)
    cvec = jnp.concatenate([al.T, db.T], axis=1)
    return wab, wad, wdt, lvec, cvec


def kernel(x, c, ctx, c_ctx, w_mod, b_mod, w_in, conv_w, a_log, dt_bias, dn_norm_w, w_dn_out, w_fmix,
           w_f_out, w_out, ln_g, ln_b):
    depth = w_mod.shape[0]
    assert depth == 1, "single-layer configuration"
    bsz, n, _ = x.shape
    nctx = ctx.shape[1]
    assert n % (4 * CHUNK) == 0 and nctx % CHUNK == 0 and bsz <= 8
    alpha = (2 * depth) ** 0.25
    tm = min(512, n)
    tmc = min(256, nctx)

    cc = jnp.zeros((16, D_MODEL), F32).at[:bsz].set(c).at[8].set(c_ctx)
    mod3 = _modulation(cc, w_mod[0], b_mod[0].reshape(1, -1)).reshape(16, 1, 3 * D_MODEL)

    w = w_in[0]
    o0, o1, o2, o3, o4, o5, o6 = (int(v) for v in np.cumsum(
        (F_WIDTH, F_WIDTH, 3 * DN_WIDTH, DN_WIDTH, AUX_GROUP, AUX_GROUP, D_MODEL)))
    wf = w[:, :o1].astype(BF16)
    wqkv = w[:, o1:o2].astype(BF16)
    wg = w[:, o2:o3].astype(BF16)
    wr = w[:, o5:].astype(BF16)
    wab, wad, wdt, lvec, cvec = _aux_weights(w[:, o3:o4], w[:, o4:o5], a_log[0], dt_bias[0])
    er, ec = _pos_tables(n // GRID_W)

    lblk, ublk = _chunk_sum_matrices(tm)
    fv, fg, qkv_x, sg, rf, rd, aux_x, gct_x = _inproj(
        x, mod3, lambda b: b, (er, ec), (wf, wqkv, wg, wr, wab, wad, wdt, conv_w[0]), (lvec, cvec, lblk, ublk),
        tm=tm, latent=True)
    lblk_c, ublk_c = _chunk_sum_matrices(tmc)
    qkv_c, aux_c, gct_c = _inproj(
        ctx, mod3, lambda b: 8, None, (wf, wqkv, wg, wr, wab, wad, wdt, conv_w[0]), (lvec, cvec, lblk_c, ublk_c),
        tm=tmc, latent=False)

    og = _deltanet(qkv_x, aux_x, gct_x, qkv_c, aux_c, gct_c, sg,
                   dn_norm_w[0].reshape(1, -1).astype(F32))
    mixed = _fourier(fv, _position_dft_matrices(n), _channel_dft_matrix())
    return _merge(x, mod3, er, ec, mixed, fg, og, rf, rd,
                  w_fmix[0].astype(BF16), w_f_out[0].astype(BF16), w_dn_out[0].astype(BF16),
                  w_out[0].astype(BF16), ln_g[0].reshape(1, -1), ln_b[0].reshape(1, -1),
                  tm=tm, alpha=alpha)
```

```python
import functools
import math

import numpy as np
import jax
import jax.numpy as jnp
from jax import lax
from jax.experimental import pallas as pl
from jax.experimental.pallas import tpu as pltpu

F32 = jnp.float32
BF16 = jnp.bfloat16

D_MODEL = 1024
GRID_W = 64
F_GROUPS = 4
F_GROUP_DIM = 128
F_WIDTH = F_GROUPS * F_GROUP_DIM
DN_HEADS = 8
DN_HEAD_DIM = 128
DN_WIDTH = DN_HEADS * DN_HEAD_DIM
N_DIR = 2
CHUNK = 64
EPS = 1e-6
LANES = 128
AUX_GROUP = 16
VMEM_LIMIT = 56 * 1024 * 1024


def _silu(x):
    return x * jax.nn.sigmoid(x)


def _softplus(x):
    return jnp.maximum(x, 0.0) + jnp.log1p(jnp.exp(-jnp.abs(x)))


def _dot_nt(a, b):
    return lax.dot_general(a.astype(BF16), b.astype(BF16), (((1,), (1,)), ((), ())),
                           preferred_element_type=F32)


def _split3(x):
    hi = x.astype(BF16)
    r1 = x - hi.astype(F32)
    mid = r1.astype(BF16)
    lo = (r1 - mid.astype(F32)).astype(BF16)
    return hi, mid, lo


def _layer_norm(x):
    mu = jnp.mean(x, axis=-1, keepdims=True)
    xc = x - mu
    var = jnp.mean(xc * xc, axis=-1, keepdims=True)
    return xc * lax.rsqrt(var + EPS)


def _add_pos(x, er_ref, ec_ref, t, tm):
    rows = tm // GRID_W
    half = D_MODEL // 2
    er = er_ref[pl.ds(t * rows, rows), :]
    ec = ec_ref[...]
    x3 = x.reshape(rows, GRID_W, D_MODEL)
    pe = jnp.concatenate([jnp.broadcast_to(er[:, None, :], (rows, GRID_W, half)),
                          jnp.broadcast_to(ec[None, :, :], (rows, GRID_W, half))], axis=-1)
    return (x3 + pe).reshape(tm, D_MODEL)


def _mod_kernel(c_ref, w_ref, b_ref, o_ref):
    s = _silu(c_ref[...])
    o_ref[...] = jnp.dot(s, w_ref[...], preferred_element_type=F32,
                         precision=lax.Precision.HIGHEST) + b_ref[...]


def _modulation(cc, w_mod, b_mod):
    rows = cc.shape[0]
    nblk = 3
    return pl.pallas_call(
        _mod_kernel,
        grid=(nblk,),
        in_specs=[pl.BlockSpec((rows, D_MODEL), lambda j: (0, 0)),
                  pl.BlockSpec((D_MODEL, D_MODEL), lambda j: (0, j)),
                  pl.BlockSpec((1, D_MODEL), lambda j: (0, j))],
        out_specs=pl.BlockSpec((rows, D_MODEL), lambda j: (0, j)),
        out_shape=jax.ShapeDtypeStruct((rows, 3 * D_MODEL), F32),
        compiler_params=pltpu.CompilerParams(dimension_semantics=("arbitrary",),
                                             vmem_limit_bytes=VMEM_LIMIT),
        name="mod",
    )(cc, w_mod, b_mod)


def _decay_jobs(h, wab_ref, wad_ref, wdt_ref, lvec_ref, cvec_ref, lblk_ref, ublk_ref, aux_ref, gct_ref, tm):
    st = {}
    sum3 = lambda mats: mats[0] + mats[1] + mats[2]

    def project():
        raw_b = jnp.dot(h, wab_ref[...], preferred_element_type=F32)
        raw_d = jnp.dot(h, wad_ref[...], preferred_element_type=F32)
        raw_t = lax.dot_general(wdt_ref[...], h, (((1,), (1,)), ((), ())), preferred_element_type=F32)
        st["beta"] = jax.nn.sigmoid(raw_b)
        st["g"] = -jnp.exp(lvec_ref[0:1, :]) * _softplus(raw_d + lvec_ref[1:2, :])
        st["gt"] = -jnp.exp(cvec_ref[:, 0:1]) * _softplus(raw_t + cvec_ref[:, 1:2])

    def chunk_sums():
        lblk, ublk = lblk_ref[...], ublk_ref[...]
        parts = _split3(st["g"])
        st["pre"] = sum3([jnp.dot(lblk, p, preferred_element_type=F32) for p in parts])
        parts_t = _split3(st["gt"])
        st["pre_t"] = sum3([jnp.dot(p, ublk, preferred_element_type=F32) for p in parts_t])
        st["suf_t"] = sum3([jnp.dot(p, lblk, preferred_element_type=F32) for p in parts_t])

    def outputs():
        beta, g, pre = st["beta"], st["g"], st["pre"]
        nck = tm // CHUNK
        pre3 = pre.reshape(nck, CHUNK, LANES)
        tot = jnp.broadcast_to(pre3[:, CHUNK - 1:CHUNK, :], (nck, CHUNK, LANES)).reshape(tm, LANES)
        suf = tot - pre + g
        lane = lax.broadcasted_iota(jnp.int32, (tm, LANES), 1)
        gc = jnp.where((lane % AUX_GROUP) >= DN_HEADS, suf, pre)
        egc = jnp.exp(gc)
        grp = lane // AUX_GROUP
        aux_ref[0] = jnp.where(grp == 0, beta,
                     jnp.where(grp == 1, gc,
                     jnp.where(grp == 2, egc,
                     jnp.where(grp == 3, beta * egc,
                     jnp.where(grp == 4, jnp.exp(tot - gc),
                     jnp.where(grp == 5, jnp.exp(tot), 0.0))))))
        row = lax.broadcasted_iota(jnp.int32, st["gt"].shape, 0)
        gct_ref[0] = jnp.where(row >= DN_HEADS, st["suf_t"], st["pre_t"])

    return [project, chunk_sums, outputs]


_HALO = 8
_QKV_BLOCK = 256


def _qkv_conv(h_ext, wqkv_ref, cw_ref, qkv_ref, tm, between=()):
    rows = tm + 2 * _HALO
    between = list(between)
    nb = 3 * DN_WIDTH // _QKV_BLOCK
    per_block = -(-len(between) // nb)
    for j in range(nb):
        for _ in range(per_block):
            if between:
                between.pop(0)()
        cols = slice(j * _QKV_BLOCK, (j + 1) * _QKV_BLOCK)
        p = jnp.dot(h_ext, wqkv_ref[:, cols], preferred_element_type=F32)
        w = cw_ref[:, cols]
        y = (pltpu.roll(p, 1, axis=0)[_HALO:_HALO + tm] * w[0:1, :] + p[_HALO:_HALO + tm] * w[1:2, :]
             + pltpu.roll(p, rows - 1, axis=0)[_HALO:_HALO + tm] * w[2:3, :])
        y = _silu(y)
        if j * _QKV_BLOCK < 2 * DN_WIDTH:
            gain = DN_HEAD_DIM ** -0.5 if j * _QKV_BLOCK < DN_WIDTH else 1.0
            heads = [y[:, i * DN_HEAD_DIM:(i + 1) * DN_HEAD_DIM] for i in range(_QKV_BLOCK // DN_HEAD_DIM)]
            y = jnp.concatenate([yh * (lax.rsqrt(jnp.sum(yh * yh, axis=-1, keepdims=True) + EPS) * gain)
                                 for yh in heads], axis=-1)
        qkv_ref[0, :, cols] = y.astype(BF16)
    for job in between:
        job()


def _column_jobs(h, w_ref, w_col0, act, out_ref, width):
    def job(c0):
        p = jnp.dot(h, w_ref[:, w_col0 + c0:w_col0 + c0 + _QKV_BLOCK], preferred_element_type=F32)
        out_ref[0, :, c0:c0 + _QKV_BLOCK] = act(p).astype(BF16)
    return [functools.partial(job, c0) for c0 in range(0, width, _QKV_BLOCK)]


def _inproj_kernel(*refs, tm, latent):
    if latent:
        (x_ref, xp_ref, xn_ref, mod_ref, er_ref, ec_ref, wf_ref, wqkv_ref, wg_ref, wr_ref, wab_ref, wad_ref,
         wdt_ref, cw_ref, lvec_ref, cvec_ref, lblk_ref, ublk_ref,
         fv_ref, fg_ref, qkv_ref, sg_ref, rf_ref, rd_ref, aux_ref, gct_ref) = refs
    else:
        (x_ref, xp_ref, xn_ref, mod_ref, wqkv_ref, wab_ref, wad_ref, wdt_ref, cw_ref,
         lvec_ref, cvec_ref, lblk_ref, ublk_ref,
         qkv_ref, aux_ref, gct_ref) = refs
    t = pl.program_id(1)
    nt = pl.num_programs(1)
    x = x_ref[0]
    xp = xp_ref[0]
    xn = xn_ref[0]
    if latent:
        x = _add_pos(x, er_ref, ec_ref, t, tm)
        rows = tm // GRID_W
        last = er_ref.shape[0] - 1
        erp = er_ref[pl.ds(jnp.maximum(t * rows - 1, 0), 1), :]
        ern = er_ref[pl.ds(jnp.minimum((t + 1) * rows, last), 1), :]
        xp = xp + jnp.concatenate([jnp.broadcast_to(erp, (_HALO, D_MODEL // 2)),
                                   ec_ref[GRID_W - _HALO:GRID_W, :]], axis=-1)
        xn = xn + jnp.concatenate([jnp.broadcast_to(ern, (_HALO, D_MODEL // 2)), ec_ref[0:_HALO, :]], axis=-1)
    mod = mod_ref[0]
    shift = mod[:, 0:D_MODEL]
    scale = mod[:, D_MODEL:2 * D_MODEL]
    hf = _layer_norm(x) * (1.0 + scale) + shift
    h = hf.astype(BF16)
    hp = (_layer_norm(xp) * (1.0 + scale) + shift) * jnp.where(t > 0, 1.0, 0.0)
    hn = (_layer_norm(xn) * (1.0 + scale) + shift) * jnp.where(t < nt - 1, 1.0, 0.0)
    project, chunk_sums, outputs = _decay_jobs(h, wab_ref, wad_ref, wdt_ref, lvec_ref, cvec_ref, lblk_ref,
                                               ublk_ref, aux_ref, gct_ref, tm)
    idle = lambda: None
    if latent:
        others = (_column_jobs(h, wr_ref, 0, jax.nn.sigmoid, rf_ref, D_MODEL)
                  + _column_jobs(h, wr_ref, D_MODEL, jax.nn.sigmoid, rd_ref, D_MODEL)
                  + _column_jobs(h, wg_ref, 0, _silu, sg_ref, DN_WIDTH)
                  + _column_jobs(h, wf_ref, F_WIDTH, _silu, fg_ref, F_WIDTH)
                  + _column_jobs(h, wf_ref, 0, lambda p: p, fv_ref, F_WIDTH))
        jobs = [project] + others[:3] + [chunk_sums] + others[3:6] + [outputs] + others[6:]
    else:
        jobs = [project, idle, idle, chunk_sums, idle, idle, outputs]
    _qkv_conv(jnp.concatenate([hp, hf, hn], axis=0).astype(BF16), wqkv_ref, cw_ref, qkv_ref, tm, jobs)


def _const_spec(shape):
    nd = len(shape)
    return pl.BlockSpec(shape, lambda b, t: (0,) * nd)


def _inproj(xin, mod3, mod_row_fn, tables, weights, consts, *, tm, latent):
    bsz, n, _ = xin.shape
    nt = n // tm
    wf, wqkv, wg, wr, wab, wad, wdt, cw = weights
    lvec, cvec, lblk, ublk = consts
    tok = lambda w: pl.BlockSpec((1, tm, w), lambda b, t: (b, t, 0))
    x_spec = pl.BlockSpec((1, tm, D_MODEL), lambda b, t: (b, t, 0))
    per = tm // _HALO
    xp_spec = pl.BlockSpec((1, _HALO, D_MODEL), lambda b, t: (b, jnp.maximum(t * per - 1, 0), 0))
    xn_spec = pl.BlockSpec((1, _HALO, D_MODEL), lambda b, t: (b, jnp.minimum((t + 1) * per, n // _HALO - 1), 0))
    mod_spec = pl.BlockSpec((1, 1, 3 * D_MODEL), lambda b, t: (mod_row_fn(b), 0, 0))
    gct_spec = pl.BlockSpec((1, AUX_GROUP, tm), lambda b, t: (b, 0, t))
    sds = lambda w, dt: jax.ShapeDtypeStruct((bsz, n, w), dt)
    gct_sds = jax.ShapeDtypeStruct((bsz, AUX_GROUP, n), F32)
    tail_in = [lvec, cvec, lblk, ublk]
    tail_specs = [_const_spec(a.shape) for a in tail_in]
    if latent:
        er, ec = tables
        ins = [xin, xin, xin, mod3, er, ec, wf, wqkv, wg, wr, wab, wad, wdt, cw] + tail_in
        in_specs = ([x_spec, xp_spec, xn_spec, mod_spec] + [_const_spec(a.shape) for a in ins[4:14]] + tail_specs)
        out_specs = [tok(F_WIDTH), tok(F_WIDTH), tok(3 * DN_WIDTH), tok(DN_WIDTH), tok(D_MODEL),
                     tok(D_MODEL), tok(LANES), gct_spec]
        out_shape = [sds(F_WIDTH, BF16), sds(F_WIDTH, BF16), sds(3 * DN_WIDTH, BF16), sds(DN_WIDTH, BF16),
                     sds(D_MODEL, BF16), sds(D_MODEL, BF16), sds(LANES, F32), gct_sds]
    else:
        ins = [xin, xin, xin, mod3, wqkv, wab, wad, wdt, cw] + tail_in
        in_specs = ([x_spec, xp_spec, xn_spec, mod_spec] + [_const_spec(a.shape) for a in ins[4:9]] + tail_specs)
        out_specs = [tok(3 * DN_WIDTH), tok(LANES), gct_spec]
        out_shape = [sds(3 * DN_WIDTH, BF16), sds(LANES, F32), gct_sds]
    return pl.pallas_call(
        functools.partial(_inproj_kernel, tm=tm, latent=latent),
        grid=(bsz, nt),
        in_specs=in_specs,
        out_specs=out_specs,
        out_shape=out_shape,
        compiler_params=pltpu.CompilerParams(dimension_semantics=("arbitrary", "arbitrary"),
                                             vmem_limit_bytes=VMEM_LIMIT),
        name="inproj_latent" if latent else "inproj_ctx",
    )(*ins)


def _block_diag2(m):
    lane = lax.broadcasted_iota(jnp.int32, m.shape, 1)
    z = jnp.zeros_like(m)
    return jnp.concatenate([jnp.where(lane < CHUNK, m, z), jnp.where(lane >= CHUNK, m, z)], axis=0).astype(BF16)


def _dn_masks():
    ri = lax.broadcasted_iota(jnp.int32, (CHUNK, 2 * CHUNK), 0)
    lane = lax.broadcasted_iota(jnp.int32, (CHUNK, 2 * CHUNK), 1)
    fwd = lane < CHUNK
    ci = jnp.where(fwd, lane, lane - CHUNK)
    bwd = jnp.logical_not(fwd)
    incl = (fwd & (ri >= ci)) | (bwd & (ri <= ci))
    strict = (fwd & (ri > ci)) | (bwd & (ri < ci))
    same = lambda k: (ri >> k) == (ci >> k)
    levels = [same(k + 1) & jnp.logical_not(same(k)) for k in range(1, int(math.log2(CHUNK)))]
    eye = jnp.where(ri == ci, 1.0, 0.0)
    return fwd, incl, strict, same(1), levels, eye


def _dn_side(refs, c, h):
    q_ref, k_ref, v_ref, aux_ref = refs
    rows = pl.ds(pl.multiple_of(c * CHUNK, CHUNK), CHUNK)
    aux = pltpu.roll(aux_ref[0, rows, :], (LANES - h) % LANES, axis=1)
    return q_ref[0, rows, :].astype(F32), k_ref[0, rows, :].astype(F32), v_ref[0, rows, :].astype(F32), aux


def _dn_prepare(pairs, h, scr, between=()):
    mq_s, b_s, egt_s = scr
    fwd, incl, strict, same1, levels, eye = _dn_masks()
    between = list(between)
    n_slots = 2 * len(levels) + 2
    per_slot = -(-len(between) // n_slots)

    def slot():
        for _ in range(per_slot):
            if between:
                between.pop(0)()

    col = lambda aux, j, d: aux[:, j * AUX_GROUP + d * DN_HEADS:j * AUX_GROUP + d * DN_HEADS + 1]
    zk = jnp.zeros((CHUNK, DN_HEAD_DIM), F32)
    sides, ls, decays, qks = [], [], [], []
    for (refs_f, cf, refs_b, cb, gt_ref, gt_row, _, _) in pairs:
        qa, ka, va, auxa = _dn_side(refs_f, cf, h)
        qb, kb, vb, auxb = _dn_side(refs_b, cb, h)
        gram = _dot_nt(jnp.concatenate([jnp.concatenate([ka, kb], axis=1), jnp.concatenate([qa, qb], axis=1)], axis=0),
                       jnp.concatenate([jnp.concatenate([ka, zk], axis=1), jnp.concatenate([zk, kb], axis=1)], axis=0))
        kk, qk = gram[:CHUNK], gram[CHUNK:]
        both = lambda j: jnp.where(fwd, col(auxa, j, 0), col(auxb, j, 1))
        gr = gt_ref[0, h, pl.ds(gt_row, 1), :]
        decay = jnp.exp(jnp.where(incl, both(1) - gr, -jnp.inf))
        ls.append(jnp.where(strict, kk * decay * both(0), 0.0))
        sides.append((qa, ka, va, auxa, qb, kb, vb, auxb)); decays.append(decay); qks.append(qk)
    slot()

    xs = [eye - jnp.where(same1, l, 0.0) for l in ls]
    for lvl in levels:
        ts = [jnp.dot(x.astype(BF16), _block_diag2(jnp.where(lvl, l, 0.0)), preferred_element_type=F32)
              for x, l in zip(xs, ls)]
        slot()
        xs = [x - jnp.dot(t.astype(BF16), _block_diag2(x), preferred_element_type=F32) for x, t in zip(xs, ts)]
        slot()

    def split_rows(m):
        lane = lax.broadcasted_iota(jnp.int32, m.shape, 1)
        z = jnp.zeros_like(m)
        return jnp.concatenate([jnp.where(lane < CHUNK, m, z), jnp.where(lane >= CHUNK, m, z)], axis=0).astype(BF16)

    sols = []
    for x, (qa, ka, va, auxa, qb, kb, vb, auxb) in zip(xs, sides):
        rhs = jnp.concatenate([jnp.concatenate([va * col(auxa, 0, 0), ka * col(auxa, 3, 0)], axis=-1),
                               jnp.concatenate([vb * col(auxb, 0, 1), kb * col(auxb, 3, 1)], axis=-1)], axis=0)
        sols.append(jnp.dot(split_rows(x), rhs.astype(BF16), preferred_element_type=F32).astype(BF16))
    slot()
    mbs = []
    for sol, (qa, ka, va, auxa, qb, kb, vb, auxb), qk, decay in zip(sols, sides, qks, decays):
        kdt = jnp.concatenate([ka * col(auxa, 4, 0), kb * col(auxb, 4, 1)], axis=0).T
        lhs = jnp.concatenate([kdt, qk * decay], axis=0)
        mbs.append(jnp.dot(split_rows(lhs), sol, preferred_element_type=F32))
    while between:
        between.pop(0)()
    w = DN_HEAD_DIM
    for (refs_f, cf, refs_b, cb, gt_ref, gt_row, pos, o0), mb, side in zip(pairs, mbs, sides):
        for d, (q, aux, c) in enumerate(((side[0], side[3], cf), (side[4], side[7], cb))):
            r = d * (w + CHUNK)
            b_s[d, pos] = mb[r:r + w, :w]
            mq_s[d, pos, 0:w, :] = mb[r:r + w, w:].astype(BF16)
            egt_s[d, pos] = jnp.broadcast_to(col(aux, 5, d)[0:1, :], (8, LANES))
            if o0 is not None:
                mq_s[d, pos, w:w + CHUNK, :] = (q * col(aux, 2, d) - mb[r + w:r + w + CHUNK, w:]).astype(BF16)
                o0[d][pl.ds(pl.multiple_of(c * CHUNK, CHUNK), CHUNK), :] = mb[r + w:r + w + CHUNK, :w]


def _dn_step(d, pos, scr, s_ref, o_ref=None, chunk=None):
    mq_s, b_s, egt_s = scr
    w = DN_HEAD_DIM
    s = s_ref[d]
    sb = s.astype(BF16)
    egt = egt_s[d, pos]
    s3 = (s.reshape(w // 8, 8, LANES) * egt[None]).reshape(w, LANES)
    if o_ref is None:
        ms = jnp.dot(mq_s[d, pos, 0:w, :], sb, preferred_element_type=F32)
    else:
        mqs = jnp.dot(mq_s[d, pos], sb, preferred_element_type=F32)
        ms = mqs[:w]
        rows = pl.ds(pl.multiple_of(chunk * CHUNK, CHUNK), CHUNK)
        o_ref[rows, :] = o_ref[rows, :] + mqs[w:]
    s_ref[d] = s3 - ms + b_s[d, pos]


_DN_GROUP = 8
_DN_FINISH_ROWS = 256


def _dn_kernel(qx, kx, vx, auxx, gtx, qc, kc, vc, auxc, gtc, out_ref,
               mq_s, b_s, egt_s, of_s, ob_s, s_ref, *, ncx, ncc):
    h = pl.program_id(1)
    scr = (mq_s, b_s, egt_s)
    lat = (qx, kx, vx, auxx)
    ctx = (qc, kc, vc, auxc)
    ngroups = ncx // _DN_GROUP

    def ctx_pair(p):
        return (ctx, p, ctx, ncc - 1 - p, gtc, p, p, None)

    def lat_pair(j):
        return (lat, j, lat, ncx - 1 - j, gtx, j, ncc + j, (of_s, ob_s))

    def ctx_step(p):
        _dn_step(0, p, scr, s_ref)
        _dn_step(1, p, scr, s_ref)

    def lat_step(j):
        _dn_step(0, ncc + j, scr, s_ref, of_s, j)
        _dn_step(1, ncc + j, scr, s_ref, ob_s, ncx - 1 - j)

    def lat_group(g):
        return [lat_pair(g * _DN_GROUP + i) for i in range(_DN_GROUP)]

    def lat_steps(g):
        return [functools.partial(lat_step, g * _DN_GROUP + i) for i in range(_DN_GROUP)]

    _dn_prepare([ctx_pair(p) for p in range(ncc)] + lat_group(0), h, scr)
    s_ref[...] = jnp.zeros_like(s_ref)
    if ngroups > 1:
        _dn_prepare(lat_group(1), h, scr, [functools.partial(ctx_step, p) for p in range(ncc)] + lat_steps(0))

        def body(g, carry):
            _dn_prepare(lat_group(g), h, scr, lat_steps(g - 1))
            return carry

        lax.fori_loop(2, ngroups, body, 0)
    else:
        for p in range(ncc):
            ctx_step(p)
    for step in lat_steps(ngroups - 1):
        step()

    fr = min(_DN_FINISH_ROWS, ncx * CHUNK)

    def finish(i, carry):
        rows = pl.ds(pl.multiple_of(i * fr, fr), fr)
        out_ref[0, rows, :] = (of_s[rows, :] + ob_s[rows, :]).astype(BF16)
        return carry

    lax.fori_loop(0, ncx * CHUNK // fr, finish, 0)


def _deltanet(qkv_x, aux_x, gct_x, qkv_c, aux_c, gct_c):
    bsz, n, _ = qkv_x.shape
    nctx = qkv_c.shape[1]
    ncx, ncc = n // CHUNK, nctx // CHUNK
    nct = ncx + ncc
    assert ncx % _DN_GROUP == 0

    def rows_by_position(g, nc):
        g = g.reshape(bsz, N_DIR, DN_HEADS, nc, CHUNK)
        g = jnp.stack([g[:, 0], jnp.flip(g[:, 1], axis=2)], axis=3)
        return g.reshape(bsz, DN_HEADS, nc, N_DIR * CHUNK)

    gct_x = rows_by_position(gct_x, ncx)
    gct_c = rows_by_position(gct_c, ncc)
    col = lambda rows, off: pl.BlockSpec((1, rows, DN_HEAD_DIM), lambda b, h: (b, 0, off + h))
    whole = lambda a: pl.BlockSpec((1,) + a.shape[1:], lambda b, h: (b,) + (0,) * (a.ndim - 1))
    in_specs = [col(n, 0), col(n, DN_HEADS), col(n, 2 * DN_HEADS), whole(aux_x), whole(gct_x),
                col(nctx, 0), col(nctx, DN_HEADS), col(nctx, 2 * DN_HEADS), whole(aux_c), whole(gct_c)]
    scratch = [pltpu.VMEM((N_DIR, nct, DN_HEAD_DIM + CHUNK, DN_HEAD_DIM), BF16),
               pltpu.VMEM((N_DIR, nct, DN_HEAD_DIM, DN_HEAD_DIM), F32),
               pltpu.VMEM((N_DIR, nct, 8, LANES), F32),
               pltpu.VMEM((n, DN_HEAD_DIM), F32),
               pltpu.VMEM((n, DN_HEAD_DIM), F32),
               pltpu.VMEM((N_DIR, DN_HEAD_DIM, DN_HEAD_DIM), F32)]
    return pl.pallas_call(
        functools.partial(_dn_kernel, ncx=ncx, ncc=ncc),
        grid=(bsz, DN_HEADS),
        in_specs=in_specs,
        out_specs=col(n, 0),
        out_shape=jax.ShapeDtypeStruct((bsz, n, DN_WIDTH), BF16),
        scratch_shapes=scratch,
        compiler_params=pltpu.CompilerParams(dimension_semantics=("arbitrary", "arbitrary"),
                                             vmem_limit_bytes=VMEM_LIMIT),
        name="deltanet",
    )(qkv_x, qkv_x, qkv_x, aux_x, gct_x, qkv_c, qkv_c, qkv_c, aux_c, gct_c)


_F_ROWS = 128


def _fourier_kernel(fv_ref, m_ref, r_ref, out_ref, pq_s, *, n4):
    k1 = pl.program_id(1)
    rb = min(_F_ROWS, n4)

    def combo(x, k):
        if k == 0:
            return x[0] + x[1] + x[2] + x[3], None
        if k == 2:
            return x[0] - x[1] + x[2] - x[3], None
        if k == 1:
            return x[0] - x[2], x[3] - x[1]
        return x[0] - x[2], x[1] - x[3]

    r = r_ref[...]
    for k in range(4):
        @pl.when(k1 == k)
        def _():
            def body(i, carry):
                r0 = pl.multiple_of(i * rb, rb)
                x = [fv_ref[0, pl.ds(j * n4 + r0, rb), :].astype(F32) for j in range(4)]
                d, e = combo(x, k)
                for g in range(F_GROUPS):
                    sl = slice(g * F_GROUP_DIM, (g + 1) * F_GROUP_DIM)
                    if e is None:
                        pq = jnp.dot(d[:, sl].astype(BF16), r[:F_GROUP_DIM, :], preferred_element_type=F32)
                    else:
                        de = jnp.concatenate([d[:, sl], e[:, sl]], axis=-1).astype(BF16)
                        pq = jnp.dot(de, r, preferred_element_type=F32)
                    pq_s[pl.ds(r0, rb), sl] = pq[:, :F_GROUP_DIM].astype(BF16)
                    pq_s[pl.ds(n4 + r0, rb), sl] = pq[:, F_GROUP_DIM:].astype(BF16)
                return carry
            lax.fori_loop(0, n4 // rb, body, 0)

    def rows(i, carry):
        r0 = pl.multiple_of(i * rb, rb)
        res = jnp.dot(m_ref[0, pl.ds(r0, rb), :], pq_s[...], preferred_element_type=F32)
        for g in range(F_GROUPS):
            out_ref[0, g, pl.ds(k1 + 4 * r0, rb, stride=4), :] = res[:, g * F_GROUP_DIM:(g + 1) * F_GROUP_DIM]
        return carry
    lax.fori_loop(0, n4 // rb, rows, 0)


def _fourier(fv, mcat, rmat):
    bsz, n, _ = fv.shape
    n4 = n // 4
    return pl.pallas_call(
        functools.partial(_fourier_kernel, n4=n4),
        grid=(bsz, 4),
        in_specs=[pl.BlockSpec((1, n, F_WIDTH), lambda b, k: (b, 0, 0)),
                  pl.BlockSpec((1, n4, 2 * n4), lambda b, k: (k, 0, 0)),
                  pl.BlockSpec((2 * F_GROUP_DIM, 2 * F_GROUP_DIM), lambda b, k: (0, 0))],
        out_specs=pl.BlockSpec((1, F_GROUPS, n, F_GROUP_DIM), lambda b, k: (b, 0, 0, 0)),
        out_shape=jax.ShapeDtypeStruct((bsz, F_GROUPS, n, F_GROUP_DIM), F32),
        scratch_shapes=[pltpu.VMEM((2 * n4, F_WIDTH), BF16)],
        compiler_params=pltpu.CompilerParams(dimension_semantics=("arbitrary", "arbitrary"),
                                             vmem_limit_bytes=VMEM_LIMIT),
        name="fourier",
    )(fv, mcat, rmat)


def _merge_kernel(x_ref, mod_ref, er_ref, ec_ref, mix_ref, fg_ref, od_ref, sg_ref, rf_ref, rd_ref,
                  wfm_ref, wfo_ref, wdo_ref, wo_ref, nw_ref, lng_ref, lnb_ref, out_ref, *, tm, alpha):
    t = pl.program_id(1)
    mixed = jnp.concatenate(
        [jnp.dot(mix_ref[0, g].astype(BF16), wfm_ref[g], preferred_element_type=F32)
         for g in range(F_GROUPS)], axis=-1)
    y_f = jnp.dot((mixed * fg_ref[0].astype(F32)).astype(BF16), wfo_ref[...], preferred_element_type=F32)
    od = od_ref[0].astype(F32)
    nw = nw_ref[...]
    heads = []
    for i in range(DN_HEADS):
        oh = od[:, i * DN_HEAD_DIM:(i + 1) * DN_HEAD_DIM]
        heads.append(oh * lax.rsqrt(jnp.mean(oh * oh, axis=-1, keepdims=True) + EPS) * nw)
    og = (jnp.concatenate(heads, axis=-1) * sg_ref[0].astype(F32)).astype(BF16)
    y_d = jnp.dot(og, wdo_ref[...], preferred_element_type=F32)
    m = rf_ref[0].astype(F32) * y_f + rd_ref[0].astype(F32) * y_d
    o = jnp.dot(m.astype(BF16), wo_ref[...], preferred_element_type=F32)
    gate = mod_ref[0][:, 2 * D_MODEL:3 * D_MODEL]
    x = _add_pos(x_ref[0], er_ref, ec_ref, t, tm)
    out_ref[0] = _layer_norm(alpha * x + gate * o) * lng_ref[...] + lnb_ref[...]


def _merge(x, mod3, er, ec, mixed, fg, od, sg, rf, rd, wfm, wfo, wdo, wo, norm_w, ln_g, ln_b, *, tm, alpha):
    bsz, n, _ = x.shape
    tok = lambda w: pl.BlockSpec((1, tm, w), lambda b, t: (b, t, 0))
    consts = [wfm, wfo, wdo, wo, norm_w, ln_g, ln_b]
    in_specs = [tok(D_MODEL),
                pl.BlockSpec((1, 1, 3 * D_MODEL), lambda b, t: (b, 0, 0)),
                _const_spec(er.shape), _const_spec(ec.shape),
                pl.BlockSpec((1, F_GROUPS, tm, F_GROUP_DIM), lambda b, t: (b, 0, t, 0)),
                tok(F_WIDTH), tok(DN_WIDTH), tok(DN_WIDTH), tok(D_MODEL), tok(D_MODEL)]
    in_specs += [_const_spec(a.shape) for a in consts]
    return pl.pallas_call(
        functools.partial(_merge_kernel, tm=tm, alpha=alpha),
        grid=(bsz, n // tm),
        in_specs=in_specs,
        out_specs=tok(D_MODEL),
        out_shape=jax.ShapeDtypeStruct((bsz, n, D_MODEL), F32),
        compiler_params=pltpu.CompilerParams(dimension_semantics=("arbitrary", "arbitrary"),
                                             vmem_limit_bytes=VMEM_LIMIT),
        name="merge",
    )(x, mod3, er, ec, mixed, fg, od, sg, rf, rd, *consts)


def _pos_tables(rows):
    quarter = D_MODEL // 4
    omega = 1.0 / (10000.0 ** (np.arange(quarter, dtype=np.float64) / quarter))
    pr = np.arange(rows, dtype=np.float64)[:, None] * omega
    pc = np.arange(GRID_W, dtype=np.float64)[:, None] * omega
    er = np.concatenate([np.sin(pr), np.cos(pr)], axis=-1)
    ec = np.concatenate([np.sin(pc), np.cos(pc)], axis=-1)
    return jnp.asarray(er, F32), jnp.asarray(ec, F32)


def _chunk_sum_matrices(tm):
    i = np.arange(tm)
    same = (i[:, None] // CHUNK) == (i[None, :] // CHUNK)
    lower = same & (i[:, None] >= i[None, :])
    upper = same & (i[:, None] <= i[None, :])
    return jnp.asarray(lower, BF16), jnp.asarray(upper, BF16)


def _channel_dft_matrix():
    c = np.arange(F_GROUP_DIM)
    ang = 2.0 * np.pi * ((c[:, None] * c[None, :]) % F_GROUP_DIM) / F_GROUP_DIM
    cc, sc = np.cos(ang), np.sin(ang)
    r = np.block([[cc, sc], [sc, -cc]]) / math.sqrt(F_GROUP_DIM)
    return jnp.asarray(r, F32).astype(BF16)


def _position_dft_matrices(n):
    n4 = n // 4
    k1 = jnp.arange(4, dtype=jnp.int32)[:, None, None]
    k2 = jnp.arange(n4, dtype=jnp.int32)[None, :, None]
    n2 = jnp.arange(n4, dtype=jnp.int32)[None, None, :]
    ang = ((n2 * (k1 + 4 * k2)) % n).astype(F32) * (2.0 * math.pi / n)
    scale = 1.0 / math.sqrt(n)
    return jnp.concatenate([jnp.cos(ang) * scale, -jnp.sin(ang) * scale], axis=-1).astype(BF16)


def _aux_weights(w_beta, w_decay, a_log, dt_bias):
    reps = LANES // AUX_GROUP
    wab = jnp.tile(w_beta, (1, reps)).astype(BF16)
    wad = jnp.tile(w_decay, (1, reps)).astype(BF16)
    wdt = w_decay.T.astype(BF16)
    al = a_log.reshape(1, AUX_GROUP).astype(F32)
    db = dt_bias.reshape(1, AUX_GROUP).astype(F32)
    lvec = jnp.concatenate([jnp.tile(al, (1, reps)), jnp.tile(db, (1, reps)),
                            jnp.zeros((6, LANES), F32)], axis=0)
    cvec = jnp.concatenate([al.T, db.T], axis=1)
    return wab, wad, wdt, lvec, cvec


def kernel(x, c, ctx, c_ctx, w_mod, b_mod, w_in, conv_w, a_log, dt_bias, dn_norm_w, w_dn_out, w_fmix,
           w_f_out, w_out, ln_g, ln_b):
    depth = w_mod.shape[0]
    assert depth == 1, "single-layer configuration"
    bsz, n, _ = x.shape
    nctx = ctx.shape[1]
    assert n % (4 * CHUNK) == 0 and nctx % CHUNK == 0 and bsz <= 8
    alpha = (2 * depth) ** 0.25
    tm = min(512, n)
    tmc = min(256, nctx)

    cc = jnp.zeros((16, D_MODEL), F32).at[:bsz].set(c).at[8].set(c_ctx)
    mod3 = _modulation(cc, w_mod[0], b_mod[0].reshape(1, -1)).reshape(16, 1, 3 * D_MODEL)

    w = w_in[0]
    o0, o1, o2, o3, o4, o5, o6 = (int(v) for v in np.cumsum(
        (F_WIDTH, F_WIDTH, 3 * DN_WIDTH, DN_WIDTH, AUX_GROUP, AUX_GROUP, D_MODEL)))
    wf = w[:, :o1].astype(BF16)
    wqkv = w[:, o1:o2].astype(BF16)
    wg = w[:, o2:o3].astype(BF16)
    wr = w[:, o5:].astype(BF16)
    wab, wad, wdt, lvec, cvec = _aux_weights(w[:, o3:o4], w[:, o4:o5], a_log[0], dt_bias[0])
    er, ec = _pos_tables(n // GRID_W)

    lblk, ublk = _chunk_sum_matrices(tm)
    fv, fg, qkv_x, sg, rf, rd, aux_x, gct_x = _inproj(
        x, mod3, lambda b: b, (er, ec), (wf, wqkv, wg, wr, wab, wad, wdt, conv_w[0]), (lvec, cvec, lblk, ublk),
        tm=tm, latent=True)
    lblk_c, ublk_c = _chunk_sum_matrices(tmc)
    qkv_c, aux_c, gct_c = _inproj(
        ctx, mod3, lambda b: 8, None, (wf, wqkv, wg, wr, wab, wad, wdt, conv_w[0]), (lvec, cvec, lblk_c, ublk_c),
        tm=tmc, latent=False)

    od = _deltanet(qkv_x, aux_x, gct_x, qkv_c, aux_c, gct_c)
    mixed = _fourier(fv, _position_dft_matrices(n), _channel_dft_matrix())
    return _merge(x, mod3, er, ec, mixed, fg, od, sg, rf, rd,
                  w_fmix[0].astype(BF16), w_f_out[0].astype(BF16), w_dn_out[0].astype(BF16),
                  w_out[0].astype(BF16), dn_norm_w[0].reshape(1, -1).astype(F32),
                  ln_g[0].reshape(1, -1), ln_b[0].reshape(1, -1), tm=tm, alpha=alpha)
```

```python
import functools
import math

import numpy as np
import jax
import jax.numpy as jnp
from jax import lax
from jax.experimental import pallas as pl
from jax.experimental.pallas import tpu as pltpu

F32 = jnp.float32
BF16 = jnp.bfloat16

D_MODEL = 1024
GRID_W = 64
F_GROUPS = 4
F_GROUP_DIM = 128
F_WIDTH = F_GROUPS * F_GROUP_DIM
DN_HEADS = 8
DN_HEAD_DIM = 128
DN_WIDTH = DN_HEADS * DN_HEAD_DIM
N_DIR = 2
CHUNK = 64
EPS = 1e-6
LANES = 128
AUX_GROUP = 16
VMEM_LIMIT = 56 * 1024 * 1024


def _silu(x):
    return x * jax.nn.sigmoid(x)


def _softplus(x):
    return jnp.maximum(x, 0.0) + jnp.log1p(jnp.exp(-jnp.abs(x)))


def _dot_nt(a, b):
    return lax.dot_general(a.astype(BF16), b.astype(BF16), (((1,), (1,)), ((), ())),
                           preferred_element_type=F32)


def _split3(x):
    hi = x.astype(BF16)
    r1 = x - hi.astype(F32)
    mid = r1.astype(BF16)
    lo = (r1 - mid.astype(F32)).astype(BF16)
    return hi, mid, lo


def _layer_norm(x):
    mu = jnp.mean(x, axis=-1, keepdims=True)
    xc = x - mu
    var = jnp.mean(xc * xc, axis=-1, keepdims=True)
    return xc * lax.rsqrt(var + EPS)


def _add_pos(x, er_ref, ec_ref, t, tm):
    rows = tm // GRID_W
    half = D_MODEL // 2
    er = er_ref[pl.ds(t * rows, rows), :]
    ec = ec_ref[...]
    x3 = x.reshape(rows, GRID_W, D_MODEL)
    pe = jnp.concatenate([jnp.broadcast_to(er[:, None, :], (rows, GRID_W, half)),
                          jnp.broadcast_to(ec[None, :, :], (rows, GRID_W, half))], axis=-1)
    return (x3 + pe).reshape(tm, D_MODEL)


def _mod_kernel(c_ref, w_ref, b_ref, o_ref):
    s = _silu(c_ref[...])
    o_ref[...] = jnp.dot(s, w_ref[...], preferred_element_type=F32,
                         precision=lax.Precision.HIGHEST) + b_ref[...]


def _modulation(cc, w_mod, b_mod):
    rows = cc.shape[0]
    nblk = 3
    return pl.pallas_call(
        _mod_kernel,
        grid=(nblk,),
        in_specs=[pl.BlockSpec((rows, D_MODEL), lambda j: (0, 0)),
                  pl.BlockSpec((D_MODEL, D_MODEL), lambda j: (0, j)),
                  pl.BlockSpec((1, D_MODEL), lambda j: (0, j))],
        out_specs=pl.BlockSpec((rows, D_MODEL), lambda j: (0, j)),
        out_shape=jax.ShapeDtypeStruct((rows, 3 * D_MODEL), F32),
        compiler_params=pltpu.CompilerParams(dimension_semantics=("arbitrary",),
                                             vmem_limit_bytes=VMEM_LIMIT),
        name="mod",
    )(cc, w_mod, b_mod)


def _decay_jobs(h, wab_ref, wad_ref, wdt_ref, lvec_ref, cvec_ref, lblk_ref, ublk_ref, aux_ref, gct_ref, tm):
    st = {}
    sum3 = lambda mats: mats[0] + mats[1] + mats[2]

    def col_project():
        raw_b = jnp.dot(h, wab_ref[...], preferred_element_type=F32)
        raw_d = jnp.dot(h, wad_ref[...], preferred_element_type=F32)
        st["beta"] = jax.nn.sigmoid(raw_b)
        st["g"] = -jnp.exp(lvec_ref[0:1, :]) * _softplus(raw_d + lvec_ref[1:2, :])

    def col_sums():
        lblk = lblk_ref[...]
        st["pre"] = sum3([jnp.dot(lblk, p, preferred_element_type=F32) for p in _split3(st["g"])])

    def col_output():
        beta, g, pre = st["beta"], st["g"], st["pre"]
        nck = tm // CHUNK
        pre3 = pre.reshape(nck, CHUNK, LANES)
        tot = jnp.broadcast_to(pre3[:, CHUNK - 1:CHUNK, :], (nck, CHUNK, LANES)).reshape(tm, LANES)
        suf = tot - pre + g
        lane = lax.broadcasted_iota(jnp.int32, (tm, LANES), 1)
        gc = jnp.where((lane % AUX_GROUP) >= DN_HEADS, suf, pre)
        egc = jnp.exp(gc)
        grp = lane // AUX_GROUP
        aux_ref[0] = jnp.where(grp == 0, beta,
                     jnp.where(grp == 1, gc,
                     jnp.where(grp == 2, egc,
                     jnp.where(grp == 3, beta * egc,
                     jnp.where(grp == 4, jnp.exp(tot - gc),
                     jnp.where(grp == 5, jnp.exp(tot), 0.0))))))

    def row_project():
        raw_t = lax.dot_general(wdt_ref[...], h, (((1,), (1,)), ((), ())), preferred_element_type=F32)
        st["gt"] = -jnp.exp(cvec_ref[:, 0:1]) * _softplus(raw_t + cvec_ref[:, 1:2])

    def row_sums():
        lblk, ublk = lblk_ref[...], ublk_ref[...]
        parts = _split3(st["gt"])
        st["pre_t"] = sum3([jnp.dot(p, ublk, preferred_element_type=F32) for p in parts])
        st["suf_t"] = sum3([jnp.dot(p, lblk, preferred_element_type=F32) for p in parts])

    def row_output():
        row = lax.broadcasted_iota(jnp.int32, st["gt"].shape, 0)
        gct_ref[0] = jnp.where(row >= DN_HEADS, st["suf_t"], st["pre_t"])

    return [col_project, col_sums, col_output], [row_project, row_sums, row_output]


_HALO = 8
_QKV_BLOCK = 256


def _qkv_conv(h_ext, wqkv_ref, cw_ref, qkv_ref, tm, between=()):
    rows = tm + 2 * _HALO
    between = list(between)
    nb = 3 * DN_WIDTH // _QKV_BLOCK
    per_block = -(-len(between) // nb)
    for j in range(nb):
        for _ in range(per_block):
            if between:
                between.pop(0)()
        cols = slice(j * _QKV_BLOCK, (j + 1) * _QKV_BLOCK)
        p = jnp.dot(h_ext, wqkv_ref[:, cols], preferred_element_type=F32)
        w = cw_ref[:, cols]
        y = (pltpu.roll(p, 1, axis=0)[_HALO:_HALO + tm] * w[0:1, :] + p[_HALO:_HALO + tm] * w[1:2, :]
             + pltpu.roll(p, rows - 1, axis=0)[_HALO:_HALO + tm] * w[2:3, :])
        y = _silu(y)
        if j * _QKV_BLOCK < 2 * DN_WIDTH:
            gain = DN_HEAD_DIM ** -0.5 if j * _QKV_BLOCK < DN_WIDTH else 1.0
            heads = [y[:, i * DN_HEAD_DIM:(i + 1) * DN_HEAD_DIM] for i in range(_QKV_BLOCK // DN_HEAD_DIM)]
            y = jnp.concatenate([yh * (lax.rsqrt(jnp.sum(yh * yh, axis=-1, keepdims=True) + EPS) * gain)
                                 for yh in heads], axis=-1)
        qkv_ref[0, :, cols] = y.astype(BF16)
    for job in between:
        job()


def _inproj_kernel(*refs, tm, latent):
    if latent:
        (x_ref, xp_ref, xn_ref, mod_ref, er_ref, ec_ref, wf_ref, wqkv_ref, wg_ref, wr_ref, wab_ref, wad_ref,
         wdt_ref, cw_ref, lvec_ref, cvec_ref, lblk_ref, ublk_ref,
         fv_ref, fg_ref, qkv_ref, sg_ref, rf_ref, rd_ref, aux_ref, gct_ref) = refs
    else:
        (x_ref, xp_ref, xn_ref, mod_ref, wqkv_ref, wab_ref, wad_ref, wdt_ref, cw_ref,
         lvec_ref, cvec_ref, lblk_ref, ublk_ref,
         qkv_ref, aux_ref, gct_ref) = refs
    t = pl.program_id(1)
    nt = pl.num_programs(1)
    x = x_ref[0]
    xp = xp_ref[0]
    xn = xn_ref[0]
    if latent:
        x = _add_pos(x, er_ref, ec_ref, t, tm)
        rows = tm // GRID_W
        last = er_ref.shape[0] - 1
        erp = er_ref[pl.ds(jnp.maximum(t * rows - 1, 0), 1), :]
        ern = er_ref[pl.ds(jnp.minimum((t + 1) * rows, last), 1), :]
        xp = xp + jnp.concatenate([jnp.broadcast_to(erp, (_HALO, D_MODEL // 2)),
                                   ec_ref[GRID_W - _HALO:GRID_W, :]], axis=-1)
        xn = xn + jnp.concatenate([jnp.broadcast_to(ern, (_HALO, D_MODEL // 2)), ec_ref[0:_HALO, :]], axis=-1)
    mod = mod_ref[0]
    shift = mod[:, 0:D_MODEL]
    scale = mod[:, D_MODEL:2 * D_MODEL]
    hf = _layer_norm(x) * (1.0 + scale) + shift
    h = hf.astype(BF16)
    hp = (_layer_norm(xp) * (1.0 + scale) + shift) * jnp.where(t > 0, 1.0, 0.0)
    hn = (_layer_norm(xn) * (1.0 + scale) + shift) * jnp.where(t < nt - 1, 1.0, 0.0)
    col_chain, row_chain = _decay_jobs(h, wab_ref, wad_ref, wdt_ref, lvec_ref, cvec_ref, lblk_ref, ublk_ref,
                                       aux_ref, gct_ref, tm)
    h_ext = jnp.concatenate([hp, hf, hn], axis=0).astype(BF16)
    if latent:
        _qkv_conv(h_ext, wqkv_ref, cw_ref, qkv_ref, tm)
        pf = jnp.dot(h, wf_ref[...], preferred_element_type=F32)
        fv_ref[0] = pf[:, :F_WIDTH].astype(BF16)
        fg_ref[0] = _silu(pf[:, F_WIDTH:]).astype(BF16)
        sg_ref[0] = _silu(jnp.dot(h, wg_ref[...], preferred_element_type=F32)).astype(BF16)
        pr = jnp.dot(h, wr_ref[...], preferred_element_type=F32)
        rf_ref[0] = jax.nn.sigmoid(pr[:, :D_MODEL]).astype(BF16)
        rd_ref[0] = jax.nn.sigmoid(pr[:, D_MODEL:]).astype(BF16)
        for job in col_chain + row_chain:
            job()
    else:
        jobs = [j for stage in zip(col_chain, row_chain) for j in stage + (lambda: None, lambda: None)]
        _qkv_conv(h_ext, wqkv_ref, cw_ref, qkv_ref, tm, jobs)


def _const_spec(shape):
    nd = len(shape)
    return pl.BlockSpec(shape, lambda b, t: (0,) * nd)


def _inproj(xin, mod3, mod_row_fn, tables, weights, consts, *, tm, latent):
    bsz, n, _ = xin.shape
    nt = n // tm
    wf, wqkv, wg, wr, wab, wad, wdt, cw = weights
    lvec, cvec, lblk, ublk = consts
    tok = lambda w: pl.BlockSpec((1, tm, w), lambda b, t: (b, t, 0))
    x_spec = pl.BlockSpec((1, tm, D_MODEL), lambda b, t: (b, t, 0))
    per = tm // _HALO
    xp_spec = pl.BlockSpec((1, _HALO, D_MODEL), lambda b, t: (b, jnp.maximum(t * per - 1, 0), 0))
    xn_spec = pl.BlockSpec((1, _HALO, D_MODEL), lambda b, t: (b, jnp.minimum((t + 1) * per, n // _HALO - 1), 0))
    mod_spec = pl.BlockSpec((1, 1, 3 * D_MODEL), lambda b, t: (mod_row_fn(b), 0, 0))
    gct_spec = pl.BlockSpec((1, AUX_GROUP, tm), lambda b, t: (b, 0, t))
    sds = lambda w, dt: jax.ShapeDtypeStruct((bsz, n, w), dt)
    gct_sds = jax.ShapeDtypeStruct((bsz, AUX_GROUP, n), F32)
    tail_in = [lvec, cvec, lblk, ublk]
    tail_specs = [_const_spec(a.shape) for a in tail_in]
    if latent:
        er, ec = tables
        ins = [xin, xin, xin, mod3, er, ec, wf, wqkv, wg, wr, wab, wad, wdt, cw] + tail_in
        in_specs = ([x_spec, xp_spec, xn_spec, mod_spec] + [_const_spec(a.shape) for a in ins[4:14]] + tail_specs)
        out_specs = [tok(F_WIDTH), tok(F_WIDTH), tok(3 * DN_WIDTH), tok(DN_WIDTH), tok(D_MODEL),
                     tok(D_MODEL), tok(LANES), gct_spec]
        out_shape = [sds(F_WIDTH, BF16), sds(F_WIDTH, BF16), sds(3 * DN_WIDTH, BF16), sds(DN_WIDTH, BF16),
                     sds(D_MODEL, BF16), sds(D_MODEL, BF16), sds(LANES, F32), gct_sds]
    else:
        ins = [xin, xin, xin, mod3, wqkv, wab, wad, wdt, cw] + tail_in
        in_specs = ([x_spec, xp_spec, xn_spec, mod_spec] + [_const_spec(a.shape) for a in ins[4:9]] + tail_specs)
        out_specs = [tok(3 * DN_WIDTH), tok(LANES), gct_spec]
        out_shape = [sds(3 * DN_WIDTH, BF16), sds(LANES, F32), gct_sds]
    return pl.pallas_call(
        functools.partial(_inproj_kernel, tm=tm, latent=latent),
        grid=(bsz, nt),
        in_specs=in_specs,
        out_specs=out_specs,
        out_shape=out_shape,
        compiler_params=pltpu.CompilerParams(dimension_semantics=("arbitrary", "arbitrary"),
                                             vmem_limit_bytes=VMEM_LIMIT),
        name="inproj_latent" if latent else "inproj_ctx",
    )(*ins)


def _block_diag2(m):
    lane = lax.broadcasted_iota(jnp.int32, m.shape, 1)
    z = jnp.zeros_like(m)
    return jnp.concatenate([jnp.where(lane < CHUNK, m, z), jnp.where(lane >= CHUNK, m, z)], axis=0).astype(BF16)


def _dn_masks():
    ri = lax.broadcasted_iota(jnp.int32, (CHUNK, 2 * CHUNK), 0)
    lane = lax.broadcasted_iota(jnp.int32, (CHUNK, 2 * CHUNK), 1)
    fwd = lane < CHUNK
    ci = jnp.where(fwd, lane, lane - CHUNK)
    bwd = jnp.logical_not(fwd)
    incl = (fwd & (ri >= ci)) | (bwd & (ri <= ci))
    strict = (fwd & (ri > ci)) | (bwd & (ri < ci))
    same = lambda k: (ri >> k) == (ci >> k)
    levels = [same(k + 1) & jnp.logical_not(same(k)) for k in range(1, int(math.log2(CHUNK)))]
    eye = jnp.where(ri == ci, 1.0, 0.0)
    return fwd, incl, strict, same(1), levels, eye


def _dn_side(refs, c, h):
    q_ref, k_ref, v_ref, aux_ref = refs
    rows = pl.ds(pl.multiple_of(c * CHUNK, CHUNK), CHUNK)
    aux = pltpu.roll(aux_ref[0, rows, :], (LANES - h) % LANES, axis=1)
    return q_ref[0, rows, :].astype(F32), k_ref[0, rows, :].astype(F32), v_ref[0, rows, :].astype(F32), aux


def _dn_prepare(pairs, h, scr, between=()):
    mq_s, b_s, egt_s = scr
    fwd, incl, strict, same1, levels, eye = _dn_masks()
    between = list(between)
    n_slots = 2 * len(levels) + 2
    per_slot = -(-len(between) // n_slots)

    def slot():
        for _ in range(per_slot):
            if between:
                between.pop(0)()

    col = lambda aux, j, d: aux[:, j * AUX_GROUP + d * DN_HEADS:j * AUX_GROUP + d * DN_HEADS + 1]
    zk = jnp.zeros((CHUNK, DN_HEAD_DIM), F32)
    sides, ls, decays, qks = [], [], [], []
    for (refs_f, cf, refs_b, cb, gt_ref, gt_row, _, _) in pairs:
        qa, ka, va, auxa = _dn_side(refs_f, cf, h)
        qb, kb, vb, auxb = _dn_side(refs_b, cb, h)
        gram = _dot_nt(jnp.concatenate([jnp.concatenate([ka, kb], axis=1), jnp.concatenate([qa, qb], axis=1)], axis=0),
                       jnp.concatenate([jnp.concatenate([ka, zk], axis=1), jnp.concatenate([zk, kb], axis=1)], axis=0))
        kk, qk = gram[:CHUNK], gram[CHUNK:]
        both = lambda j: jnp.where(fwd, col(auxa, j, 0), col(auxb, j, 1))
        gr = gt_ref[0, h, pl.ds(gt_row, 1), :]
        decay = jnp.exp(jnp.where(incl, both(1) - gr, -jnp.inf))
        ls.append(jnp.where(strict, kk * decay * both(0), 0.0))
        sides.append((qa, ka, va, auxa, qb, kb, vb, auxb)); decays.append(decay); qks.append(qk)
    slot()

    xs = [eye - jnp.where(same1, l, 0.0) for l in ls]
    for lvl in levels:
        ts = [jnp.dot(x.astype(BF16), _block_diag2(jnp.where(lvl, l, 0.0)), preferred_element_type=F32)
              for x, l in zip(xs, ls)]
        slot()
        xs = [x - jnp.dot(t.astype(BF16), _block_diag2(x), preferred_element_type=F32) for x, t in zip(xs, ts)]
        slot()

    def split_rows(m):
        lane = lax.broadcasted_iota(jnp.int32, m.shape, 1)
        z = jnp.zeros_like(m)
        return jnp.concatenate([jnp.where(lane < CHUNK, m, z), jnp.where(lane >= CHUNK, m, z)], axis=0).astype(BF16)

    sols = []
    for x, (qa, ka, va, auxa, qb, kb, vb, auxb) in zip(xs, sides):
        rhs = jnp.concatenate([jnp.concatenate([va * col(auxa, 0, 0), ka * col(auxa, 3, 0)], axis=-1),
                               jnp.concatenate([vb * col(auxb, 0, 1), kb * col(auxb, 3, 1)], axis=-1)], axis=0)
        sols.append(jnp.dot(split_rows(x), rhs.astype(BF16), preferred_element_type=F32).astype(BF16))
    slot()
    mbs = []
    for sol, (qa, ka, va, auxa, qb, kb, vb, auxb), qk, decay in zip(sols, sides, qks, decays):
        kdt = jnp.concatenate([ka * col(auxa, 4, 0), kb * col(auxb, 4, 1)], axis=0).T
        lhs = jnp.concatenate([kdt, qk * decay], axis=0)
        mbs.append(jnp.dot(split_rows(lhs), sol, preferred_element_type=F32))
    while between:
        between.pop(0)()
    w = DN_HEAD_DIM
    for (refs_f, cf, refs_b, cb, gt_ref, gt_row, pos, o0), mb, side in zip(pairs, mbs, sides):
        for d, (q, aux, c) in enumerate(((side[0], side[3], cf), (side[4], side[7], cb))):
            r = d * (w + CHUNK)
            b_s[d, pos] = mb[r:r + w, :w]
            mq_s[d, pos, 0:w, :] = mb[r:r + w, w:].astype(BF16)
            egt_s[d, pos] = jnp.broadcast_to(col(aux, 5, d)[0:1, :], (8, LANES))
            if o0 is not None:
                mq_s[d, pos, w:w + CHUNK, :] = (q * col(aux, 2, d) - mb[r + w:r + w + CHUNK, w:]).astype(BF16)
                o0[d][pl.ds(pl.multiple_of(c * CHUNK, CHUNK), CHUNK), :] = mb[r + w:r + w + CHUNK, :w]


def _dn_step(d, pos, scr, s_ref, o_ref=None, chunk=None):
    mq_s, b_s, egt_s = scr
    w = DN_HEAD_DIM
    s = s_ref[d]
    sb = s.astype(BF16)
    egt = egt_s[d, pos]
    s3 = (s.reshape(w // 8, 8, LANES) * egt[None]).reshape(w, LANES)
    if o_ref is None:
        ms = jnp.dot(mq_s[d, pos, 0:w, :], sb, preferred_element_type=F32)
    else:
        mqs = jnp.dot(mq_s[d, pos], sb, preferred_element_type=F32)
        ms = mqs[:w]
        rows = pl.ds(pl.multiple_of(chunk * CHUNK, CHUNK), CHUNK)
        o_ref[rows, :] = o_ref[rows, :] + mqs[w:]
    s_ref[d] = s3 - ms + b_s[d, pos]


_DN_GROUP = 8
_DN_FINISH_ROWS = 256


def _dn_kernel(qx, kx, vx, auxx, gtx, qc, kc, vc, auxc, gtc, out_ref,
               mq_s, b_s, egt_s, of_s, ob_s, s_ref, *, ncx, ncc):
    h = pl.program_id(1)
    scr = (mq_s, b_s, egt_s)
    lat = (qx, kx, vx, auxx)
    ctx = (qc, kc, vc, auxc)
    ngroups = ncx // _DN_GROUP

    def ctx_pair(p):
        return (ctx, p, ctx, ncc - 1 - p, gtc, p, p, None)

    def lat_pair(j):
        return (lat, j, lat, ncx - 1 - j, gtx, j, ncc + j, (of_s, ob_s))

    def ctx_step(p):
        _dn_step(0, p, scr, s_ref)
        _dn_step(1, p, scr, s_ref)

    def lat_step(j):
        _dn_step(0, ncc + j, scr, s_ref, of_s, j)
        _dn_step(1, ncc + j, scr, s_ref, ob_s, ncx - 1 - j)

    def lat_group(g):
        return [lat_pair(g * _DN_GROUP + i) for i in range(_DN_GROUP)]

    def lat_steps(g):
        return [functools.partial(lat_step, g * _DN_GROUP + i) for i in range(_DN_GROUP)]

    _dn_prepare([ctx_pair(p) for p in range(ncc)] + lat_group(0), h, scr)
    s_ref[...] = jnp.zeros_like(s_ref)
    if ngroups > 1:
        _dn_prepare(lat_group(1), h, scr, [functools.partial(ctx_step, p) for p in range(ncc)] + lat_steps(0))

        def body(g, carry):
            _dn_prepare(lat_group(g), h, scr, lat_steps(g - 1))
            return carry

        lax.fori_loop(2, ngroups, body, 0)
    else:
        for p in range(ncc):
            ctx_step(p)
    for step in lat_steps(ngroups - 1):
        step()

    fr = min(_DN_FINISH_ROWS, ncx * CHUNK)

    def finish(i, carry):
        rows = pl.ds(pl.multiple_of(i * fr, fr), fr)
        out_ref[0, rows, :] = (of_s[rows, :] + ob_s[rows, :]).astype(BF16)
        return carry

    lax.fori_loop(0, ncx * CHUNK // fr, finish, 0)


def _deltanet(qkv_x, aux_x, gct_x, qkv_c, aux_c, gct_c):
    bsz, n, _ = qkv_x.shape
    nctx = qkv_c.shape[1]
    ncx, ncc = n // CHUNK, nctx // CHUNK
    nct = ncx + ncc
    assert ncx % _DN_GROUP == 0

    def rows_by_position(g, nc):
        g = g.reshape(bsz, N_DIR, DN_HEADS, nc, CHUNK)
        g = jnp.stack([g[:, 0], jnp.flip(g[:, 1], axis=2)], axis=3)
        return g.reshape(bsz, DN_HEADS, nc, N_DIR * CHUNK)

    gct_x = rows_by_position(gct_x, ncx)
    gct_c = rows_by_position(gct_c, ncc)
    col = lambda rows, off: pl.BlockSpec((1, rows, DN_HEAD_DIM), lambda b, h: (b, 0, off + h))
    whole = lambda a: pl.BlockSpec((1,) + a.shape[1:], lambda b, h: (b,) + (0,) * (a.ndim - 1))
    in_specs = [col(n, 0), col(n, DN_HEADS), col(n, 2 * DN_HEADS), whole(aux_x), whole(gct_x),
                col(nctx, 0), col(nctx, DN_HEADS), col(nctx, 2 * DN_HEADS), whole(aux_c), whole(gct_c)]
    scratch = [pltpu.VMEM((N_DIR, nct, DN_HEAD_DIM + CHUNK, DN_HEAD_DIM), BF16),
               pltpu.VMEM((N_DIR, nct, DN_HEAD_DIM, DN_HEAD_DIM), F32),
               pltpu.VMEM((N_DIR, nct, 8, LANES), F32),
               pltpu.VMEM((n, DN_HEAD_DIM), F32),
               pltpu.VMEM((n, DN_HEAD_DIM), F32),
               pltpu.VMEM((N_DIR, DN_HEAD_DIM, DN_HEAD_DIM), F32)]
    return pl.pallas_call(
        functools.partial(_dn_kernel, ncx=ncx, ncc=ncc),
        grid=(bsz, DN_HEADS),
        in_specs=in_specs,
        out_specs=col(n, 0),
        out_shape=jax.ShapeDtypeStruct((bsz, n, DN_WIDTH), BF16),
        scratch_shapes=scratch,
        compiler_params=pltpu.CompilerParams(dimension_semantics=("arbitrary", "arbitrary"),
                                             vmem_limit_bytes=VMEM_LIMIT),
        name="deltanet",
    )(qkv_x, qkv_x, qkv_x, aux_x, gct_x, qkv_c, qkv_c, qkv_c, aux_c, gct_c)


_F_ROWS = 1024


def _fourier_kernel(fv_ref, m_ref, r_ref, out_ref, pq_s, *, n4):
    k1 = pl.program_id(1)
    rb = min(_F_ROWS, n4)

    def combo(x, k):
        if k == 0:
            return x[0] + x[1] + x[2] + x[3], None
        if k == 2:
            return x[0] - x[1] + x[2] - x[3], None
        if k == 1:
            return x[0] - x[2], x[3] - x[1]
        return x[0] - x[2], x[1] - x[3]

    r = r_ref[...]
    for k in range(4):
        @pl.when(k1 == k)
        def _():
            def body(i, carry):
                r0 = pl.multiple_of(i * rb, rb)
                x = [fv_ref[0, pl.ds(j * n4 + r0, rb), :].astype(F32) for j in range(4)]
                d, e = combo(x, k)
                for g in range(F_GROUPS):
                    sl = slice(g * F_GROUP_DIM, (g + 1) * F_GROUP_DIM)
                    if e is None:
                        pq = jnp.dot(d[:, sl].astype(BF16), r[:F_GROUP_DIM, :], preferred_element_type=F32)
                    else:
                        de = jnp.concatenate([d[:, sl], e[:, sl]], axis=-1).astype(BF16)
                        pq = jnp.dot(de, r, preferred_element_type=F32)
                    pq_s[pl.ds(r0, rb), sl] = pq[:, :F_GROUP_DIM].astype(BF16)
                    pq_s[pl.ds(n4 + r0, rb), sl] = pq[:, F_GROUP_DIM:].astype(BF16)
                return carry
            lax.fori_loop(0, n4 // rb, body, 0)

    def rows(i, carry):
        r0 = pl.multiple_of(i * rb, rb)
        res = jnp.dot(m_ref[0, pl.ds(r0, rb), :], pq_s[...], preferred_element_type=F32)
        for g in range(F_GROUPS):
            out_ref[0, g, pl.ds(k1 + 4 * r0, rb, stride=4), :] = res[:, g * F_GROUP_DIM:(g + 1) * F_GROUP_DIM]
        return carry
    lax.fori_loop(0, n4 // rb, rows, 0)


def _fourier(fv, mcat, rmat):
    bsz, n, _ = fv.shape
    n4 = n // 4
    return pl.pallas_call(
        functools.partial(_fourier_kernel, n4=n4),
        grid=(bsz, 4),
        in_specs=[pl.BlockSpec((1, n, F_WIDTH), lambda b, k: (b, 0, 0)),
                  pl.BlockSpec((1, n4, 2 * n4), lambda b, k: (k, 0, 0)),
                  pl.BlockSpec((2 * F_GROUP_DIM, 2 * F_GROUP_DIM), lambda b, k: (0, 0))],
        out_specs=pl.BlockSpec((1, F_GROUPS, n, F_GROUP_DIM), lambda b, k: (b, 0, 0, 0)),
        out_shape=jax.ShapeDtypeStruct((bsz, F_GROUPS, n, F_GROUP_DIM), F32),
        scratch_shapes=[pltpu.VMEM((2 * n4, F_WIDTH), BF16)],
        compiler_params=pltpu.CompilerParams(dimension_semantics=("arbitrary", "arbitrary"),
                                             vmem_limit_bytes=VMEM_LIMIT),
        name="fourier",
    )(fv, mcat, rmat)


def _merge_kernel(x_ref, mod_ref, er_ref, ec_ref, mix_ref, fg_ref, od_ref, sg_ref, rf_ref, rd_ref,
                  wfm_ref, wfo_ref, wdo_ref, wo_ref, nw_ref, lng_ref, lnb_ref, out_ref, *, tm, alpha):
    t = pl.program_id(1)
    mixed = jnp.concatenate(
        [jnp.dot(mix_ref[0, g].astype(BF16), wfm_ref[g], preferred_element_type=F32)
         for g in range(F_GROUPS)], axis=-1)
    y_f = jnp.dot((mixed * fg_ref[0].astype(F32)).astype(BF16), wfo_ref[...], preferred_element_type=F32)
    od = od_ref[0].astype(F32)
    nw = nw_ref[...]
    heads = []
    for i in range(DN_HEADS):
        oh = od[:, i * DN_HEAD_DIM:(i + 1) * DN_HEAD_DIM]
        heads.append(oh * lax.rsqrt(jnp.mean(oh * oh, axis=-1, keepdims=True) + EPS) * nw)
    og = (jnp.concatenate(heads, axis=-1) * sg_ref[0].astype(F32)).astype(BF16)
    y_d = jnp.dot(og, wdo_ref[...], preferred_element_type=F32)
    m = rf_ref[0].astype(F32) * y_f + rd_ref[0].astype(F32) * y_d
    o = jnp.dot(m.astype(BF16), wo_ref[...], preferred_element_type=F32)
    gate = mod_ref[0][:, 2 * D_MODEL:3 * D_MODEL]
    x = _add_pos(x_ref[0], er_ref, ec_ref, t, tm)
    out_ref[0] = _layer_norm(alpha * x + gate * o) * lng_ref[...] + lnb_ref[...]


def _merge(x, mod3, er, ec, mixed, fg, od, sg, rf, rd, wfm, wfo, wdo, wo, norm_w, ln_g, ln_b, *, tm, alpha):
    bsz, n, _ = x.shape
    tok = lambda w: pl.BlockSpec((1, tm, w), lambda b, t: (b, t, 0))
    consts = [wfm, wfo, wdo, wo, norm_w, ln_g, ln_b]
    in_specs = [tok(D_MODEL),
                pl.BlockSpec((1, 1, 3 * D_MODEL), lambda b, t: (b, 0, 0)),
                _const_spec(er.shape), _const_spec(ec.shape),
                pl.BlockSpec((1, F_GROUPS, tm, F_GROUP_DIM), lambda b, t: (b, 0, t, 0)),
                tok(F_WIDTH), tok(DN_WIDTH), tok(DN_WIDTH), tok(D_MODEL), tok(D_MODEL)]
    in_specs += [_const_spec(a.shape) for a in consts]
    return pl.pallas_call(
        functools.partial(_merge_kernel, tm=tm, alpha=alpha),
        grid=(bsz, n // tm),
        in_specs=in_specs,
        out_specs=tok(D_MODEL),
        out_shape=jax.ShapeDtypeStruct((bsz, n, D_MODEL), F32),
        compiler_params=pltpu.CompilerParams(dimension_semantics=("arbitrary", "arbitrary"),
                                             vmem_limit_bytes=VMEM_LIMIT),
        name="merge",
    )(x, mod3, er, ec, mixed, fg, od, sg, rf, rd, *consts)


def _pos_tables(rows):
    quarter = D_MODEL // 4
    omega = 1.0 / (10000.0 ** (np.arange(quarter, dtype=np.float64) / quarter))
    pr = np.arange(rows, dtype=np.float64)[:, None] * omega
    pc = np.arange(GRID_W, dtype=np.float64)[:, None] * omega
    er = np.concatenate([np.sin(pr), np.cos(pr)], axis=-1)
    ec = np.concatenate([np.sin(pc), np.cos(pc)], axis=-1)
    return jnp.asarray(er, F32), jnp.asarray(ec, F32)


def _chunk_sum_matrices(tm):
    i = np.arange(tm)
    same = (i[:, None] // CHUNK) == (i[None, :] // CHUNK)
    lower = same & (i[:, None] >= i[None, :])
    upper = same & (i[:, None] <= i[None, :])
    return jnp.asarray(lower, BF16), jnp.asarray(upper, BF16)


def _channel_dft_matrix():
    c = np.arange(F_GROUP_DIM)
    ang = 2.0 * np.pi * ((c[:, None] * c[None, :]) % F_GROUP_DIM) / F_GROUP_DIM
    cc, sc = np.cos(ang), np.sin(ang)
    r = np.block([[cc, sc], [sc, -cc]]) / math.sqrt(F_GROUP_DIM)
    return jnp.asarray(r, F32).astype(BF16)


def _position_dft_matrices(n):
    n4 = n // 4
    idx = jnp.arange(n4, dtype=jnp.int32)
    beta = ((idx[:, None] * idx[None, :]) % n4).astype(F32) * (2.0 * math.pi / n4)
    alpha = (jnp.arange(4, dtype=jnp.int32)[:, None] * idx[None, :]).astype(F32) * (2.0 * math.pi / n)
    cb, sb = jnp.cos(beta)[None], jnp.sin(beta)[None]
    ca, sa = jnp.cos(alpha)[:, None, :], jnp.sin(alpha)[:, None, :]
    scale = 1.0 / math.sqrt(n)
    return jnp.concatenate([(ca * cb - sa * sb) * scale, (sa * cb + ca * sb) * -scale], axis=-1).astype(BF16)


def _aux_weights(w_beta, w_decay, a_log, dt_bias):
    reps = LANES // AUX_GROUP
    wab = jnp.tile(w_beta, (1, reps)).astype(BF16)
    wad = jnp.tile(w_decay, (1, reps)).astype(BF16)
    wdt = w_decay.T.astype(BF16)
    al = a_log.reshape(1, AUX_GROUP).astype(F32)
    db = dt_bias.reshape(1, AUX_GROUP).astype(F32)
    lvec = jnp.concatenate([jnp.tile(al, (1, reps)), jnp.tile(db, (1, reps)),
                            jnp.zeros((6, LANES), F32)], axis=0)
    cvec = jnp.concatenate([al.T, db.T], axis=1)
    return wab, wad, wdt, lvec, cvec


def kernel(x, c, ctx, c_ctx, w_mod, b_mod, w_in, conv_w, a_log, dt_bias, dn_norm_w, w_dn_out, w_fmix,
           w_f_out, w_out, ln_g, ln_b):
    depth = w_mod.shape[0]
    assert depth == 1, "single-layer configuration"
    bsz, n, _ = x.shape
    nctx = ctx.shape[1]
    assert n % (4 * CHUNK) == 0 and nctx % CHUNK == 0 and bsz <= 8
    alpha = (2 * depth) ** 0.25
    tm = min(512, n)
    tmc = min(256, nctx)

    cc = jnp.zeros((16, D_MODEL), F32).at[:bsz].set(c).at[8].set(c_ctx)
    mod3 = _modulation(cc, w_mod[0], b_mod[0].reshape(1, -1)).reshape(16, 1, 3 * D_MODEL)

    w = w_in[0]
    o0, o1, o2, o3, o4, o5, o6 = (int(v) for v in np.cumsum(
        (F_WIDTH, F_WIDTH, 3 * DN_WIDTH, DN_WIDTH, AUX_GROUP, AUX_GROUP, D_MODEL)))
    wf = w[:, :o1].astype(BF16)
    wqkv = w[:, o1:o2].astype(BF16)
    wg = w[:, o2:o3].astype(BF16)
    wr = w[:, o5:].astype(BF16)
    wab, wad, wdt, lvec, cvec = _aux_weights(w[:, o3:o4], w[:, o4:o5], a_log[0], dt_bias[0])
    er, ec = _pos_tables(n // GRID_W)

    lblk, ublk = _chunk_sum_matrices(tm)
    fv, fg, qkv_x, sg, rf, rd, aux_x, gct_x = _inproj(
        x, mod3, lambda b: b, (er, ec), (wf, wqkv, wg, wr, wab, wad, wdt, conv_w[0]), (lvec, cvec, lblk, ublk),
        tm=tm, latent=True)
    lblk_c, ublk_c = _chunk_sum_matrices(tmc)
    qkv_c, aux_c, gct_c = _inproj(
        ctx, mod3, lambda b: 8, None, (wf, wqkv, wg, wr, wab, wad, wdt, conv_w[0]), (lvec, cvec, lblk_c, ublk_c),
        tm=tmc, latent=False)

    od = _deltanet(qkv_x, aux_x, gct_x, qkv_c, aux_c, gct_c)
    mixed = _fourier(fv, _position_dft_matrices(n), _channel_dft_matrix())
    return _merge(x, mod3, er, ec, mixed, fg, od, sg, rf, rd,
                  w_fmix[0].astype(BF16), w_f_out[0].astype(BF16), w_dn_out[0].astype(BF16),
                  w_out[0].astype(BF16), dn_norm_w[0].reshape(1, -1).astype(F32),
                  ln_g[0].reshape(1, -1), ln_b[0].reshape(1, -1), tm=tm, alpha=alpha)
```

```python
import functools
import math

import numpy as np
import jax
import jax.numpy as jnp
from jax import lax
from jax.experimental import pallas as pl
from jax.experimental.pallas import tpu as pltpu

F32 = jnp.float32
BF16 = jnp.bfloat16

D_MODEL = 1024
GRID_W = 64
F_GROUPS = 4
F_GROUP_DIM = 128
F_WIDTH = F_GROUPS * F_GROUP_DIM
DN_HEADS = 8
DN_HEAD_DIM = 128
DN_WIDTH = DN_HEADS * DN_HEAD_DIM
N_DIR = 2
CHUNK = 64
EPS = 1e-6
LANES = 128
AUX_GROUP = 16
VMEM_LIMIT = 56 * 1024 * 1024


def _silu(x):
    return x * jax.nn.sigmoid(x)


def _softplus(x):
    return jnp.maximum(x, 0.0) + jnp.log1p(jnp.exp(-jnp.abs(x)))


def _dot_nt(a, b):
    return lax.dot_general(a.astype(BF16), b.astype(BF16), (((1,), (1,)), ((), ())),
                           preferred_element_type=F32)


def _split3(x):
    hi = x.astype(BF16)
    r1 = x - hi.astype(F32)
    mid = r1.astype(BF16)
    lo = (r1 - mid.astype(F32)).astype(BF16)
    return hi, mid, lo


def _layer_norm(x):
    mu = jnp.mean(x, axis=-1, keepdims=True)
    xc = x - mu
    var = jnp.mean(xc * xc, axis=-1, keepdims=True)
    return xc * lax.rsqrt(var + EPS)


def _add_pos(x, er_ref, ec_ref, t, tm):
    rows = tm // GRID_W
    half = D_MODEL // 2
    er = er_ref[pl.ds(t * rows, rows), :]
    ec = ec_ref[...]
    x3 = x.reshape(rows, GRID_W, D_MODEL)
    pe = jnp.concatenate([jnp.broadcast_to(er[:, None, :], (rows, GRID_W, half)),
                          jnp.broadcast_to(ec[None, :, :], (rows, GRID_W, half))], axis=-1)
    return (x3 + pe).reshape(tm, D_MODEL)


def _mod_kernel(c_ref, w_ref, b_ref, o_ref):
    s = _silu(c_ref[...])
    o_ref[...] = jnp.dot(s, w_ref[...], preferred_element_type=F32,
                         precision=lax.Precision.HIGHEST) + b_ref[...]


def _modulation(cc, w_mod, b_mod):
    rows = cc.shape[0]
    nblk = 3
    return pl.pallas_call(
        _mod_kernel,
        grid=(nblk,),
        in_specs=[pl.BlockSpec((rows, D_MODEL), lambda j: (0, 0)),
                  pl.BlockSpec((D_MODEL, D_MODEL), lambda j: (0, j)),
                  pl.BlockSpec((1, D_MODEL), lambda j: (0, j))],
        out_specs=pl.BlockSpec((rows, D_MODEL), lambda j: (0, j)),
        out_shape=jax.ShapeDtypeStruct((rows, 3 * D_MODEL), F32),
        compiler_params=pltpu.CompilerParams(dimension_semantics=("arbitrary",),
                                             vmem_limit_bytes=VMEM_LIMIT),
        name="mod",
    )(cc, w_mod, b_mod)


def _decay_jobs(h, wab_ref, wad_ref, wdt_ref, lvec_ref, cvec_ref, lblk_ref, ublk_ref, aux_ref, gct_ref, tm):
    st = {}
    sum3 = lambda mats: mats[0] + mats[1] + mats[2]

    def col_project():
        raw_b = jnp.dot(h, wab_ref[...], preferred_element_type=F32)
        raw_d = jnp.dot(h, wad_ref[...], preferred_element_type=F32)
        st["beta"] = jax.nn.sigmoid(raw_b)
        st["g"] = -jnp.exp(lvec_ref[0:1, :]) * _softplus(raw_d + lvec_ref[1:2, :])

    def col_sums():
        lblk = lblk_ref[...]
        st["pre"] = sum3([jnp.dot(lblk, p, preferred_element_type=F32) for p in _split3(st["g"])])

    def col_output():
        beta, g, pre = st["beta"], st["g"], st["pre"]
        nck = tm // CHUNK
        pre3 = pre.reshape(nck, CHUNK, LANES)
        tot = jnp.broadcast_to(pre3[:, CHUNK - 1:CHUNK, :], (nck, CHUNK, LANES)).reshape(tm, LANES)
        suf = tot - pre + g
        lane = lax.broadcasted_iota(jnp.int32, (tm, LANES), 1)
        gc = jnp.where((lane % AUX_GROUP) >= DN_HEADS, suf, pre)
        egc = jnp.exp(gc)
        grp = lane // AUX_GROUP
        aux_ref[0] = jnp.where(grp == 0, beta,
                     jnp.where(grp == 1, gc,
                     jnp.where(grp == 2, egc,
                     jnp.where(grp == 3, beta * egc,
                     jnp.where(grp == 4, jnp.exp(tot - gc),
                     jnp.where(grp == 5, jnp.exp(tot), 0.0))))))

    def row_project():
        raw_t = lax.dot_general(wdt_ref[...], h, (((1,), (1,)), ((), ())), preferred_element_type=F32)
        st["gt"] = -jnp.exp(cvec_ref[:, 0:1]) * _softplus(raw_t + cvec_ref[:, 1:2])

    def row_sums():
        lblk, ublk = lblk_ref[...], ublk_ref[...]
        parts = _split3(st["gt"])
        st["pre_t"] = sum3([jnp.dot(p, ublk, preferred_element_type=F32) for p in parts])
        st["suf_t"] = sum3([jnp.dot(p, lblk, preferred_element_type=F32) for p in parts])

    def row_output():
        row = lax.broadcasted_iota(jnp.int32, st["gt"].shape, 0)
        gct_ref[0] = jnp.where(row >= DN_HEADS, st["suf_t"], st["pre_t"])

    return [col_project, col_sums, col_output], [row_project, row_sums, row_output]


_HALO = 8
_QKV_BLOCK = 256


def _qkv_conv(h_ext, wqkv_ref, cw_ref, qkv_ref, tm, between=()):
    rows = tm + 2 * _HALO
    between = list(between)
    nb = 3 * DN_WIDTH // _QKV_BLOCK
    per_block = -(-len(between) // nb)
    for j in range(nb):
        for _ in range(per_block):
            if between:
                between.pop(0)()
        cols = slice(j * _QKV_BLOCK, (j + 1) * _QKV_BLOCK)
        p = jnp.dot(h_ext, wqkv_ref[:, cols], preferred_element_type=F32)
        w = cw_ref[:, cols]
        y = (pltpu.roll(p, 1, axis=0)[_HALO:_HALO + tm] * w[0:1, :] + p[_HALO:_HALO + tm] * w[1:2, :]
             + pltpu.roll(p, rows - 1, axis=0)[_HALO:_HALO + tm] * w[2:3, :])
        y = _silu(y)
        if j * _QKV_BLOCK < 2 * DN_WIDTH:
            gain = DN_HEAD_DIM ** -0.5 if j * _QKV_BLOCK < DN_WIDTH else 1.0
            heads = [y[:, i * DN_HEAD_DIM:(i + 1) * DN_HEAD_DIM] for i in range(_QKV_BLOCK // DN_HEAD_DIM)]
            y = jnp.concatenate([yh * (lax.rsqrt(jnp.sum(yh * yh, axis=-1, keepdims=True) + EPS) * gain)
                                 for yh in heads], axis=-1)
        qkv_ref[0, :, cols] = y.astype(BF16)
    for job in between:
        job()


def _inproj_kernel(*refs, tm, latent):
    if latent:
        (x_ref, xp_ref, xn_ref, mod_ref, er_ref, ec_ref, wf_ref, wqkv_ref, wg_ref, wr_ref, wab_ref, wad_ref,
         wdt_ref, cw_ref, lvec_ref, cvec_ref, lblk_ref, ublk_ref,
         fv_ref, fg_ref, qkv_ref, sg_ref, rf_ref, rd_ref, aux_ref, gct_ref) = refs
    else:
        (x_ref, xp_ref, xn_ref, mod_ref, wqkv_ref, wab_ref, wad_ref, wdt_ref, cw_ref,
         lvec_ref, cvec_ref, lblk_ref, ublk_ref,
         qkv_ref, aux_ref, gct_ref) = refs
    t = pl.program_id(1)
    nt = pl.num_programs(1)
    x = x_ref[0]
    xp = xp_ref[0]
    xn = xn_ref[0]
    if latent:
        x = _add_pos(x, er_ref, ec_ref, t, tm)
        rows = tm // GRID_W
        last = er_ref.shape[0] - 1
        erp = er_ref[pl.ds(jnp.maximum(t * rows - 1, 0), 1), :]
        ern = er_ref[pl.ds(jnp.minimum((t + 1) * rows, last), 1), :]
        xp = xp + jnp.concatenate([jnp.broadcast_to(erp, (_HALO, D_MODEL // 2)),
                                   ec_ref[GRID_W - _HALO:GRID_W, :]], axis=-1)
        xn = xn + jnp.concatenate([jnp.broadcast_to(ern, (_HALO, D_MODEL // 2)), ec_ref[0:_HALO, :]], axis=-1)
    mod = mod_ref[0]
    shift = mod[:, 0:D_MODEL]
    scale = mod[:, D_MODEL:2 * D_MODEL]
    hf = _layer_norm(x) * (1.0 + scale) + shift
    h = hf.astype(BF16)
    hp = (_layer_norm(xp) * (1.0 + scale) + shift) * jnp.where(t > 0, 1.0, 0.0)
    hn = (_layer_norm(xn) * (1.0 + scale) + shift) * jnp.where(t < nt - 1, 1.0, 0.0)
    col_chain, row_chain = _decay_jobs(h, wab_ref, wad_ref, wdt_ref, lvec_ref, cvec_ref, lblk_ref, ublk_ref,
                                       aux_ref, gct_ref, tm)
    h_ext = jnp.concatenate([hp, hf, hn], axis=0).astype(BF16)
    if latent:
        _qkv_conv(h_ext, wqkv_ref, cw_ref, qkv_ref, tm)
        pf = jnp.dot(h, wf_ref[...], preferred_element_type=F32)
        fv_ref[0] = pf[:, :F_WIDTH].astype(BF16)
        fg_ref[0] = _silu(pf[:, F_WIDTH:]).astype(BF16)
        sg_ref[0] = _silu(jnp.dot(h, wg_ref[...], preferred_element_type=F32)).astype(BF16)
        pr = jnp.dot(h, wr_ref[...], preferred_element_type=F32)
        rf_ref[0] = jax.nn.sigmoid(pr[:, :D_MODEL]).astype(BF16)
        rd_ref[0] = jax.nn.sigmoid(pr[:, D_MODEL:]).astype(BF16)
        for job in col_chain + row_chain:
            job()
    else:
        jobs = [j for stage in zip(col_chain, row_chain) for j in stage + (lambda: None, lambda: None)]
        _qkv_conv(h_ext, wqkv_ref, cw_ref, qkv_ref, tm, jobs)


def _const_spec(shape):
    nd = len(shape)
    return pl.BlockSpec(shape, lambda b, t: (0,) * nd)


def _inproj(xin, mod3, mod_row_fn, tables, weights, consts, *, tm, latent):
    bsz, n, _ = xin.shape
    nt = n // tm
    wf, wqkv, wg, wr, wab, wad, wdt, cw = weights
    lvec, cvec, lblk, ublk = consts
    tok = lambda w: pl.BlockSpec((1, tm, w), lambda b, t: (b, t, 0))
    x_spec = pl.BlockSpec((1, tm, D_MODEL), lambda b, t: (b, t, 0))
    per = tm // _HALO
    xp_spec = pl.BlockSpec((1, _HALO, D_MODEL), lambda b, t: (b, jnp.maximum(t * per - 1, 0), 0))
    xn_spec = pl.BlockSpec((1, _HALO, D_MODEL), lambda b, t: (b, jnp.minimum((t + 1) * per, n // _HALO - 1), 0))
    mod_spec = pl.BlockSpec((1, 1, 3 * D_MODEL), lambda b, t: (mod_row_fn(b), 0, 0))
    gct_spec = pl.BlockSpec((1, AUX_GROUP, tm), lambda b, t: (b, 0, t))
    sds = lambda w, dt: jax.ShapeDtypeStruct((bsz, n, w), dt)
    gct_sds = jax.ShapeDtypeStruct((bsz, AUX_GROUP, n), F32)
    tail_in = [lvec, cvec, lblk, ublk]
    tail_specs = [_const_spec(a.shape) for a in tail_in]
    if latent:
        er, ec = tables
        ins = [xin, xin, xin, mod3, er, ec, wf, wqkv, wg, wr, wab, wad, wdt, cw] + tail_in
        in_specs = ([x_spec, xp_spec, xn_spec, mod_spec] + [_const_spec(a.shape) for a in ins[4:14]] + tail_specs)
        out_specs = [tok(F_WIDTH), tok(F_WIDTH), tok(3 * DN_WIDTH), tok(DN_WIDTH), tok(D_MODEL),
                     tok(D_MODEL), tok(LANES), gct_spec]
        out_shape = [sds(F_WIDTH, BF16), sds(F_WIDTH, BF16), sds(3 * DN_WIDTH, BF16), sds(DN_WIDTH, BF16),
                     sds(D_MODEL, BF16), sds(D_MODEL, BF16), sds(LANES, F32), gct_sds]
    else:
        ins = [xin, xin, xin, mod3, wqkv, wab, wad, wdt, cw] + tail_in
        in_specs = ([x_spec, xp_spec, xn_spec, mod_spec] + [_const_spec(a.shape) for a in ins[4:9]] + tail_specs)
        out_specs = [tok(3 * DN_WIDTH), tok(LANES), gct_spec]
        out_shape = [sds(3 * DN_WIDTH, BF16), sds(LANES, F32), gct_sds]
    return pl.pallas_call(
        functools.partial(_inproj_kernel, tm=tm, latent=latent),
        grid=(bsz, nt),
        in_specs=in_specs,
        out_specs=out_specs,
        out_shape=out_shape,
        compiler_params=pltpu.CompilerParams(dimension_semantics=("arbitrary", "arbitrary"),
                                             vmem_limit_bytes=VMEM_LIMIT),
        name="inproj_latent" if latent else "inproj_ctx",
    )(*ins)


def _block_diag2(m):
    lane = lax.broadcasted_iota(jnp.int32, m.shape, 1)
    z = jnp.zeros_like(m)
    return jnp.concatenate([jnp.where(lane < CHUNK, m, z), jnp.where(lane >= CHUNK, m, z)], axis=0).astype(BF16)


def _dn_masks():
    ri = lax.broadcasted_iota(jnp.int32, (CHUNK, 2 * CHUNK), 0)
    lane = lax.broadcasted_iota(jnp.int32, (CHUNK, 2 * CHUNK), 1)
    fwd = lane < CHUNK
    ci = jnp.where(fwd, lane, lane - CHUNK)
    bwd = jnp.logical_not(fwd)
    incl = (fwd & (ri >= ci)) | (bwd & (ri <= ci))
    strict = (fwd & (ri > ci)) | (bwd & (ri < ci))
    same = lambda k: (ri >> k) == (ci >> k)
    levels = [same(k + 1) & jnp.logical_not(same(k)) for k in range(1, int(math.log2(CHUNK)))]
    eye = jnp.where(ri == ci, 1.0, 0.0)
    return fwd, incl, strict, same(1), levels, eye


def _dn_side(refs, c, h):
    q_ref, k_ref, v_ref, aux_ref = refs
    rows = pl.ds(pl.multiple_of(c * CHUNK, CHUNK), CHUNK)
    aux = pltpu.roll(aux_ref[0, rows, :], (LANES - h) % LANES, axis=1)
    return q_ref[0, rows, :].astype(F32), k_ref[0, rows, :].astype(F32), v_ref[0, rows, :].astype(F32), aux


def _dn_prepare(pairs, h, scr, between=()):
    mq_s, b_s, egt_s = scr
    fwd, incl, strict, same1, levels, eye = _dn_masks()
    between = list(between)
    n_slots = 2 * len(levels) + 2
    per_slot = -(-len(between) // n_slots)

    def slot():
        for _ in range(per_slot):
            if between:
                between.pop(0)()

    col = lambda aux, j, d: aux[:, j * AUX_GROUP + d * DN_HEADS:j * AUX_GROUP + d * DN_HEADS + 1]
    zk = jnp.zeros((CHUNK, DN_HEAD_DIM), F32)
    sides, ls, decays, qks = [], [], [], []
    for (refs_f, cf, refs_b, cb, gt_ref, _, _) in pairs:
        qa, ka, va, auxa = _dn_side(refs_f, cf, h)
        qb, kb, vb, auxb = _dn_side(refs_b, cb, h)
        gram = _dot_nt(jnp.concatenate([jnp.concatenate([ka, kb], axis=1), jnp.concatenate([qa, qb], axis=1)], axis=0),
                       jnp.concatenate([jnp.concatenate([ka, zk], axis=1), jnp.concatenate([zk, kb], axis=1)], axis=0))
        kk, qk = gram[:CHUNK], gram[CHUNK:]
        both = lambda j: jnp.where(fwd, col(auxa, j, 0), col(auxb, j, 1))
        gr = jnp.concatenate([gt_ref[0, h, pl.ds(cf, 1), :], gt_ref[0, DN_HEADS + h, pl.ds(cb, 1), :]],
                             axis=1)
        decay = jnp.exp(jnp.where(incl, both(1) - gr, -jnp.inf))
        ls.append(jnp.where(strict, kk * decay * both(0), 0.0))
        sides.append((qa, ka, va, auxa, qb, kb, vb, auxb)); decays.append(decay); qks.append(qk)
    slot()

    xs = [eye - jnp.where(same1, l, 0.0) for l in ls]
    for lvl in levels:
        ts = [jnp.dot(x.astype(BF16), _block_diag2(jnp.where(lvl, l, 0.0)), preferred_element_type=F32)
              for x, l in zip(xs, ls)]
        slot()
        xs = [x - jnp.dot(t.astype(BF16), _block_diag2(x), preferred_element_type=F32) for x, t in zip(xs, ts)]
        slot()

    def split_rows(m):
        lane = lax.broadcasted_iota(jnp.int32, m.shape, 1)
        z = jnp.zeros_like(m)
        return jnp.concatenate([jnp.where(lane < CHUNK, m, z), jnp.where(lane >= CHUNK, m, z)], axis=0).astype(BF16)

    sols = []
    for x, (qa, ka, va, auxa, qb, kb, vb, auxb) in zip(xs, sides):
        rhs = jnp.concatenate([jnp.concatenate([va * col(auxa, 0, 0), ka * col(auxa, 3, 0)], axis=-1),
                               jnp.concatenate([vb * col(auxb, 0, 1), kb * col(auxb, 3, 1)], axis=-1)], axis=0)
        sols.append(jnp.dot(split_rows(x), rhs.astype(BF16), preferred_element_type=F32).astype(BF16))
    slot()
    mbs = []
    for sol, (qa, ka, va, auxa, qb, kb, vb, auxb), qk, decay in zip(sols, sides, qks, decays):
        kdt = jnp.concatenate([ka * col(auxa, 4, 0), kb * col(auxb, 4, 1)], axis=0).T
        lhs = jnp.concatenate([kdt, qk * decay], axis=0)
        mbs.append(jnp.dot(split_rows(lhs), sol, preferred_element_type=F32))
    while between:
        between.pop(0)()
    w = DN_HEAD_DIM
    for (refs_f, cf, refs_b, cb, gt_ref, pos, o0), mb, side in zip(pairs, mbs, sides):
        for d, (q, aux, c) in enumerate(((side[0], side[3], cf), (side[4], side[7], cb))):
            r = d * (w + CHUNK)
            b_s[d, pos] = mb[r:r + w, :w]
            mq_s[d, pos, 0:w, :] = mb[r:r + w, w:].astype(BF16)
            egt_s[d, pos] = jnp.broadcast_to(col(aux, 5, d)[0:1, :], (8, LANES))
            if o0 is not None:
                mq_s[d, pos, w:w + CHUNK, :] = (q * col(aux, 2, d) - mb[r + w:r + w + CHUNK, w:]).astype(BF16)
                o0[d][pl.ds(pl.multiple_of(c * CHUNK, CHUNK), CHUNK), :] = mb[r + w:r + w + CHUNK, :w]


def _dn_step(d, pos, scr, s_ref, o_ref=None, chunk=None):
    mq_s, b_s, egt_s = scr
    w = DN_HEAD_DIM
    s = s_ref[d]
    sb = s.astype(BF16)
    egt = egt_s[d, pos]
    s3 = (s.reshape(w // 8, 8, LANES) * egt[None]).reshape(w, LANES)
    if o_ref is None:
        ms = jnp.dot(mq_s[d, pos, 0:w, :], sb, preferred_element_type=F32)
    else:
        mqs = jnp.dot(mq_s[d, pos], sb, preferred_element_type=F32)
        ms = mqs[:w]
        rows = pl.ds(pl.multiple_of(chunk * CHUNK, CHUNK), CHUNK)
        o_ref[rows, :] = o_ref[rows, :] + mqs[w:]
    s_ref[d] = s3 - ms + b_s[d, pos]


_DN_GROUP = 8
_DN_FINISH_ROWS = 256


def _dn_kernel(qx, kx, vx, auxx, gtx, qc, kc, vc, auxc, gtc, out_ref,
               mq_s, b_s, egt_s, of_s, ob_s, s_ref, *, ncx, ncc):
    h = pl.program_id(1)
    scr = (mq_s, b_s, egt_s)
    lat = (qx, kx, vx, auxx)
    ctx = (qc, kc, vc, auxc)
    ngroups = ncx // _DN_GROUP

    def ctx_pair(p):
        return (ctx, p, ctx, ncc - 1 - p, gtc, p, None)

    def lat_pair(j):
        return (lat, j, lat, ncx - 1 - j, gtx, ncc + j, (of_s, ob_s))

    def ctx_step(p):
        _dn_step(0, p, scr, s_ref)
        _dn_step(1, p, scr, s_ref)

    def lat_step(j):
        _dn_step(0, ncc + j, scr, s_ref, of_s, j)
        _dn_step(1, ncc + j, scr, s_ref, ob_s, ncx - 1 - j)

    def lat_group(g):
        return [lat_pair(g * _DN_GROUP + i) for i in range(_DN_GROUP)]

    def lat_steps(g):
        return [functools.partial(lat_step, g * _DN_GROUP + i) for i in range(_DN_GROUP)]

    _dn_prepare([ctx_pair(p) for p in range(ncc)] + lat_group(0), h, scr)
    s_ref[...] = jnp.zeros_like(s_ref)
    if ngroups > 1:
        _dn_prepare(lat_group(1), h, scr, [functools.partial(ctx_step, p) for p in range(ncc)] + lat_steps(0))

        def body(g, carry):
            _dn_prepare(lat_group(g), h, scr, lat_steps(g - 1))
            return carry

        lax.fori_loop(2, ngroups, body, 0)
    else:
        for p in range(ncc):
            ctx_step(p)
    for step in lat_steps(ngroups - 1):
        step()

    fr = min(_DN_FINISH_ROWS, ncx * CHUNK)

    def finish(i, carry):
        rows = pl.ds(pl.multiple_of(i * fr, fr), fr)
        out_ref[0, rows, :] = (of_s[rows, :] + ob_s[rows, :]).astype(BF16)
        return carry

    lax.fori_loop(0, ncx * CHUNK // fr, finish, 0)


def _deltanet(qkv_x, aux_x, gct_x, qkv_c, aux_c, gct_c):
    bsz, n, _ = qkv_x.shape
    nctx = qkv_c.shape[1]
    ncx, ncc = n // CHUNK, nctx // CHUNK
    nct = ncx + ncc
    assert ncx % _DN_GROUP == 0

    gct_x = gct_x.reshape(bsz, AUX_GROUP, ncx, CHUNK)
    gct_c = gct_c.reshape(bsz, AUX_GROUP, ncc, CHUNK)
    col = lambda rows, off: pl.BlockSpec((1, rows, DN_HEAD_DIM), lambda b, h: (b, 0, off + h))
    whole = lambda a: pl.BlockSpec((1,) + a.shape[1:], lambda b, h: (b,) + (0,) * (a.ndim - 1))
    in_specs = [col(n, 0), col(n, DN_HEADS), col(n, 2 * DN_HEADS), whole(aux_x), whole(gct_x),
                col(nctx, 0), col(nctx, DN_HEADS), col(nctx, 2 * DN_HEADS), whole(aux_c), whole(gct_c)]
    scratch = [pltpu.VMEM((N_DIR, nct, DN_HEAD_DIM + CHUNK, DN_HEAD_DIM), BF16),
               pltpu.VMEM((N_DIR, nct, DN_HEAD_DIM, DN_HEAD_DIM), F32),
               pltpu.VMEM((N_DIR, nct, 8, LANES), F32),
               pltpu.VMEM((n, DN_HEAD_DIM), F32),
               pltpu.VMEM((n, DN_HEAD_DIM), F32),
               pltpu.VMEM((N_DIR, DN_HEAD_DIM, DN_HEAD_DIM), F32)]
    return pl.pallas_call(
        functools.partial(_dn_kernel, ncx=ncx, ncc=ncc),
        grid=(bsz, DN_HEADS),
        in_specs=in_specs,
        out_specs=col(n, 0),
        out_shape=jax.ShapeDtypeStruct((bsz, n, DN_WIDTH), BF16),
        scratch_shapes=scratch,
        compiler_params=pltpu.CompilerParams(dimension_semantics=("arbitrary", "arbitrary"),
                                             vmem_limit_bytes=VMEM_LIMIT),
        name="deltanet",
    )(qkv_x, qkv_x, qkv_x, aux_x, gct_x, qkv_c, qkv_c, qkv_c, aux_c, gct_c)


_F_ROWS = 1024


def _fourier_kernel(fv_ref, m_ref, r_ref, out_ref, pq_s, *, n4):
    k1 = pl.program_id(1)
    rb = min(_F_ROWS, n4)

    def combo(x, k):
        if k == 0:
            return x[0] + x[1] + x[2] + x[3], None
        if k == 2:
            return x[0] - x[1] + x[2] - x[3], None
        if k == 1:
            return x[0] - x[2], x[3] - x[1]
        return x[0] - x[2], x[1] - x[3]

    r = r_ref[...]
    for k in range(4):
        @pl.when(k1 == k)
        def _():
            def body(i, carry):
                r0 = pl.multiple_of(i * rb, rb)
                x = [fv_ref[0, pl.ds(j * n4 + r0, rb), :].astype(F32) for j in range(4)]
                d, e = combo(x, k)
                for g in range(F_GROUPS):
                    sl = slice(g * F_GROUP_DIM, (g + 1) * F_GROUP_DIM)
                    if e is None:
                        pq = jnp.dot(d[:, sl].astype(BF16), r[:F_GROUP_DIM, :], preferred_element_type=F32)
                    else:
                        de = jnp.concatenate([d[:, sl], e[:, sl]], axis=-1).astype(BF16)
                        pq = jnp.dot(de, r, preferred_element_type=F32)
                    pq_s[pl.ds(r0, rb), sl] = pq[:, :F_GROUP_DIM].astype(BF16)
                    pq_s[pl.ds(n4 + r0, rb), sl] = pq[:, F_GROUP_DIM:].astype(BF16)
                return carry
            lax.fori_loop(0, n4 // rb, body, 0)

    def rows(i, carry):
        r0 = pl.multiple_of(i * rb, rb)
        res = jnp.dot(m_ref[0, pl.ds(r0, rb), :], pq_s[...], preferred_element_type=F32)
        for g in range(F_GROUPS):
            out_ref[0, g, pl.ds(k1 + 4 * r0, rb, stride=4), :] = res[:, g * F_GROUP_DIM:(g + 1) * F_GROUP_DIM]
        return carry
    lax.fori_loop(0, n4 // rb, rows, 0)


def _fourier(fv, mcat, rmat):
    bsz, n, _ = fv.shape
    n4 = n // 4
    return pl.pallas_call(
        functools.partial(_fourier_kernel, n4=n4),
        grid=(bsz, 4),
        in_specs=[pl.BlockSpec((1, n, F_WIDTH), lambda b, k: (b, 0, 0)),
                  pl.BlockSpec((1, n4, 2 * n4), lambda b, k: (k, 0, 0)),
                  pl.BlockSpec((2 * F_GROUP_DIM, 2 * F_GROUP_DIM), lambda b, k: (0, 0))],
        out_specs=pl.BlockSpec((1, F_GROUPS, n, F_GROUP_DIM), lambda b, k: (b, 0, 0, 0)),
        out_shape=jax.ShapeDtypeStruct((bsz, F_GROUPS, n, F_GROUP_DIM), F32),
        scratch_shapes=[pltpu.VMEM((2 * n4, F_WIDTH), BF16)],
        compiler_params=pltpu.CompilerParams(dimension_semantics=("arbitrary", "arbitrary"),
                                             vmem_limit_bytes=VMEM_LIMIT),
        name="fourier",
    )(fv, mcat, rmat)


def _merge_kernel(x_ref, mod_ref, er_ref, ec_ref, mix_ref, fg_ref, od_ref, sg_ref, rf_ref, rd_ref,
                  wfm_ref, wfo_ref, wdo_ref, wo_ref, nw_ref, lng_ref, lnb_ref, out_ref, *, tm, alpha):
    t = pl.program_id(1)
    mixed = jnp.concatenate(
        [jnp.dot(mix_ref[0, g].astype(BF16), wfm_ref[g], preferred_element_type=F32)
         for g in range(F_GROUPS)], axis=-1)
    y_f = jnp.dot((mixed * fg_ref[0].astype(F32)).astype(BF16), wfo_ref[...], preferred_element_type=F32)
    od = od_ref[0].astype(F32)
    nw = nw_ref[...]
    heads = []
    for i in range(DN_HEADS):
        oh = od[:, i * DN_HEAD_DIM:(i + 1) * DN_HEAD_DIM]
        heads.append(oh * lax.rsqrt(jnp.mean(oh * oh, axis=-1, keepdims=True) + EPS) * nw)
    og = (jnp.concatenate(heads, axis=-1) * sg_ref[0].astype(F32)).astype(BF16)
    y_d = jnp.dot(og, wdo_ref[...], preferred_element_type=F32)
    m = rf_ref[0].astype(F32) * y_f + rd_ref[0].astype(F32) * y_d
    o = jnp.dot(m.astype(BF16), wo_ref[...], preferred_element_type=F32)
    gate = mod_ref[0][:, 2 * D_MODEL:3 * D_MODEL]
    x = _add_pos(x_ref[0], er_ref, ec_ref, t, tm)
    out_ref[0] = _layer_norm(alpha * x + gate * o) * lng_ref[...] + lnb_ref[...]


def _merge(x, mod3, er, ec, mixed, fg, od, sg, rf, rd, wfm, wfo, wdo, wo, norm_w, ln_g, ln_b, *, tm, alpha):
    bsz, n, _ = x.shape
    tok = lambda w: pl.BlockSpec((1, tm, w), lambda b, t: (b, t, 0))
    consts = [wfm, wfo, wdo, wo, norm_w, ln_g, ln_b]
    in_specs = [tok(D_MODEL),
                pl.BlockSpec((1, 1, 3 * D_MODEL), lambda b, t: (b, 0, 0)),
                _const_spec(er.shape), _const_spec(ec.shape),
                pl.BlockSpec((1, F_GROUPS, tm, F_GROUP_DIM), lambda b, t: (b, 0, t, 0)),
                tok(F_WIDTH), tok(DN_WIDTH), tok(DN_WIDTH), tok(D_MODEL), tok(D_MODEL)]
    in_specs += [_const_spec(a.shape) for a in consts]
    return pl.pallas_call(
        functools.partial(_merge_kernel, tm=tm, alpha=alpha),
        grid=(bsz, n // tm),
        in_specs=in_specs,
        out_specs=tok(D_MODEL),
        out_shape=jax.ShapeDtypeStruct((bsz, n, D_MODEL), F32),
        compiler_params=pltpu.CompilerParams(dimension_semantics=("arbitrary", "arbitrary"),
                                             vmem_limit_bytes=VMEM_LIMIT),
        name="merge",
    )(x, mod3, er, ec, mixed, fg, od, sg, rf, rd, *consts)


def _pos_tables(rows):
    quarter = D_MODEL // 4
    omega = 1.0 / (10000.0 ** (np.arange(quarter, dtype=np.float64) / quarter))
    pr = np.arange(rows, dtype=np.float64)[:, None] * omega
    pc = np.arange(GRID_W, dtype=np.float64)[:, None] * omega
    er = np.concatenate([np.sin(pr), np.cos(pr)], axis=-1)
    ec = np.concatenate([np.sin(pc), np.cos(pc)], axis=-1)
    return jnp.asarray(er, F32), jnp.asarray(ec, F32)


def _chunk_sum_matrices(tm):
    i = np.arange(tm)
    same = (i[:, None] // CHUNK) == (i[None, :] // CHUNK)
    lower = same & (i[:, None] >= i[None, :])
    upper = same & (i[:, None] <= i[None, :])
    return jnp.asarray(lower, BF16), jnp.asarray(upper, BF16)


def _channel_dft_matrix():
    c = np.arange(F_GROUP_DIM)
    ang = 2.0 * np.pi * ((c[:, None] * c[None, :]) % F_GROUP_DIM) / F_GROUP_DIM
    cc, sc = np.cos(ang), np.sin(ang)
    r = np.block([[cc, sc], [sc, -cc]]) / math.sqrt(F_GROUP_DIM)
    return jnp.asarray(r, F32).astype(BF16)


def _position_dft_matrices(n):
    n4 = n // 4
    idx = jnp.arange(n4, dtype=jnp.int32)
    beta = ((idx[:, None] * idx[None, :]) % n4).astype(F32) * (2.0 * math.pi / n4)
    alpha = (jnp.arange(4, dtype=jnp.int32)[:, None] * idx[None, :]).astype(F32) * (2.0 * math.pi / n)
    cb, sb = jnp.cos(beta)[None], jnp.sin(beta)[None]
    ca, sa = jnp.cos(alpha)[:, None, :], jnp.sin(alpha)[:, None, :]
    scale = 1.0 / math.sqrt(n)
    return jnp.concatenate([(ca * cb - sa * sb) * scale, (sa * cb + ca * sb) * -scale], axis=-1).astype(BF16)


def _aux_weights(w_beta, w_decay, a_log, dt_bias):
    reps = LANES // AUX_GROUP
    wab = jnp.tile(w_beta, (1, reps)).astype(BF16)
    wad = jnp.tile(w_decay, (1, reps)).astype(BF16)
    wdt = w_decay.T.astype(BF16)
    al = a_log.reshape(1, AUX_GROUP).astype(F32)
    db = dt_bias.reshape(1, AUX_GROUP).astype(F32)
    lvec = jnp.concatenate([jnp.tile(al, (1, reps)), jnp.tile(db, (1, reps)),
                            jnp.zeros((6, LANES), F32)], axis=0)
    cvec = jnp.concatenate([al.T, db.T], axis=1)
    return wab, wad, wdt, lvec, cvec


def kernel(x, c, ctx, c_ctx, w_mod, b_mod, w_in, conv_w, a_log, dt_bias, dn_norm_w, w_dn_out, w_fmix,
           w_f_out, w_out, ln_g, ln_b):
    depth = w_mod.shape[0]
    assert depth == 1, "single-layer configuration"
    bsz, n, _ = x.shape
    nctx = ctx.shape[1]
    assert n % (4 * CHUNK) == 0 and nctx % CHUNK == 0 and bsz <= 8
    alpha = (2 * depth) ** 0.25
    tm = min(512, n)
    tmc = min(256, nctx)

    cc = jnp.zeros((16, D_MODEL), F32).at[:bsz].set(c).at[8].set(c_ctx)
    mod3 = _modulation(cc, w_mod[0], b_mod[0].reshape(1, -1)).reshape(16, 1, 3 * D_MODEL)

    w = w_in[0]
    o0, o1, o2, o3, o4, o5, o6 = (int(v) for v in np.cumsum(
        (F_WIDTH, F_WIDTH, 3 * DN_WIDTH, DN_WIDTH, AUX_GROUP, AUX_GROUP, D_MODEL)))
    wf = w[:, :o1].astype(BF16)
    wqkv = w[:, o1:o2].astype(BF16)
    wg = w[:, o2:o3].astype(BF16)
    wr = w[:, o5:].astype(BF16)
    wab, wad, wdt, lvec, cvec = _aux_weights(w[:, o3:o4], w[:, o4:o5], a_log[0], dt_bias[0])
    er, ec = _pos_tables(n // GRID_W)

    lblk, ublk = _chunk_sum_matrices(tm)
    fv, fg, qkv_x, sg, rf, rd, aux_x, gct_x = _inproj(
        x, mod3, lambda b: b, (er, ec), (wf, wqkv, wg, wr, wab, wad, wdt, conv_w[0]), (lvec, cvec, lblk, ublk),
        tm=tm, latent=True)
    lblk_c, ublk_c = _chunk_sum_matrices(tmc)
    qkv_c, aux_c, gct_c = _inproj(
        ctx, mod3, lambda b: 8, None, (wf, wqkv, wg, wr, wab, wad, wdt, conv_w[0]), (lvec, cvec, lblk_c, ublk_c),
        tm=tmc, latent=False)

    od = _deltanet(qkv_x, aux_x, gct_x, qkv_c, aux_c, gct_c)
    mixed = _fourier(fv, _position_dft_matrices(n), _channel_dft_matrix())
    return _merge(x, mod3, er, ec, mixed, fg, od, sg, rf, rd,
                  w_fmix[0].astype(BF16), w_f_out[0].astype(BF16), w_dn_out[0].astype(BF16),
                  w_out[0].astype(BF16), dn_norm_w[0].reshape(1, -1).astype(F32),
                  ln_g[0].reshape(1, -1), ln_b[0].reshape(1, -1), tm=tm, alpha=alpha)
```

```python
import functools
import math

import numpy as np
import jax
import jax.numpy as jnp
from jax import lax
from jax.experimental import pallas as pl
from jax.experimental.pallas import tpu as pltpu

F32 = jnp.float32
BF16 = jnp.bfloat16

D_MODEL = 1024
GRID_W = 64
F_GROUPS = 4
F_GROUP_DIM = 128
F_WIDTH = F_GROUPS * F_GROUP_DIM
DN_HEADS = 8
DN_HEAD_DIM = 128
DN_WIDTH = DN_HEADS * DN_HEAD_DIM
N_DIR = 2
CHUNK = 64
EPS = 1e-6
LANES = 128
AUX_GROUP = 16
VMEM_LIMIT = 56 * 1024 * 1024


def _silu(x):
    return x * jax.nn.sigmoid(x)


def _softplus(x):
    return jnp.maximum(x, 0.0) + jnp.log1p(jnp.exp(-jnp.abs(x)))


def _dot_nt(a, b):
    return lax.dot_general(a.astype(BF16), b.astype(BF16), (((1,), (1,)), ((), ())),
                           preferred_element_type=F32)


def _split3(x):
    hi = x.astype(BF16)
    r1 = x - hi.astype(F32)
    mid = r1.astype(BF16)
    lo = (r1 - mid.astype(F32)).astype(BF16)
    return hi, mid, lo


def _layer_norm(x):
    mu = jnp.mean(x, axis=-1, keepdims=True)
    xc = x - mu
    var = jnp.mean(xc * xc, axis=-1, keepdims=True)
    return xc * lax.rsqrt(var + EPS)


def _add_pos(x, er_ref, ec_ref, t, tm):
    rows = tm // GRID_W
    half = D_MODEL // 2
    er = er_ref[pl.ds(t * rows, rows), :]
    ec = ec_ref[...]
    x3 = x.reshape(rows, GRID_W, D_MODEL)
    pe = jnp.concatenate([jnp.broadcast_to(er[:, None, :], (rows, GRID_W, half)),
                          jnp.broadcast_to(ec[None, :, :], (rows, GRID_W, half))], axis=-1)
    return (x3 + pe).reshape(tm, D_MODEL)


def _mod_kernel(c_ref, w_ref, b_ref, o_ref):
    s = _silu(c_ref[...])
    o_ref[...] = jnp.dot(s, w_ref[...], preferred_element_type=F32,
                         precision=lax.Precision.HIGHEST) + b_ref[...]


def _modulation(cc, w_mod, b_mod):
    rows = cc.shape[0]
    nblk = 3
    return pl.pallas_call(
        _mod_kernel,
        grid=(nblk,),
        in_specs=[pl.BlockSpec((rows, D_MODEL), lambda j: (0, 0)),
                  pl.BlockSpec((D_MODEL, D_MODEL), lambda j: (0, j)),
                  pl.BlockSpec((1, D_MODEL), lambda j: (0, j))],
        out_specs=pl.BlockSpec((rows, D_MODEL), lambda j: (0, j)),
        out_shape=jax.ShapeDtypeStruct((rows, 3 * D_MODEL), F32),
        compiler_params=pltpu.CompilerParams(dimension_semantics=("arbitrary",),
                                             vmem_limit_bytes=VMEM_LIMIT),
        name="mod",
    )(cc, w_mod, b_mod)


def _decay_jobs(h, wab_ref, wad_ref, wdt_ref, lvec_ref, cvec_ref, lblk_ref, ublk_ref, aux_ref, gct_ref, tm):
    st = {}
    sum3 = lambda mats: mats[0] + mats[1] + mats[2]

    def col_project():
        raw_b = jnp.dot(h, wab_ref[...], preferred_element_type=F32)
        raw_d = jnp.dot(h, wad_ref[...], preferred_element_type=F32)
        st["beta"] = jax.nn.sigmoid(raw_b)
        st["g"] = -jnp.exp(lvec_ref[0:1, :]) * _softplus(raw_d + lvec_ref[1:2, :])

    def col_sums():
        lblk = lblk_ref[...]
        st["pre"] = sum3([jnp.dot(lblk, p, preferred_element_type=F32) for p in _split3(st["g"])])

    def col_output():
        beta, g, pre = st["beta"], st["g"], st["pre"]
        nck = tm // CHUNK
        pre3 = pre.reshape(nck, CHUNK, LANES)
        tot = jnp.broadcast_to(pre3[:, CHUNK - 1:CHUNK, :], (nck, CHUNK, LANES)).reshape(tm, LANES)
        suf = tot - pre + g
        lane = lax.broadcasted_iota(jnp.int32, (tm, LANES), 1)
        gc = jnp.where((lane % AUX_GROUP) >= DN_HEADS, suf, pre)
        egc = jnp.exp(gc)
        grp = lane // AUX_GROUP
        aux_ref[0] = jnp.where(grp == 0, beta,
                     jnp.where(grp == 1, gc,
                     jnp.where(grp == 2, egc,
                     jnp.where(grp == 3, beta * egc,
                     jnp.where(grp == 4, jnp.exp(tot - gc),
                     jnp.where(grp == 5, jnp.exp(tot), 0.0))))))

    def row_project():
        raw_t = lax.dot_general(wdt_ref[...], h, (((1,), (1,)), ((), ())), preferred_element_type=F32)
        st["gt"] = -jnp.exp(cvec_ref[:, 0:1]) * _softplus(raw_t + cvec_ref[:, 1:2])

    def row_sums():
        lblk, ublk = lblk_ref[...], ublk_ref[...]
        parts = _split3(st["gt"])
        st["pre_t"] = sum3([jnp.dot(p, ublk, preferred_element_type=F32) for p in parts])
        st["suf_t"] = sum3([jnp.dot(p, lblk, preferred_element_type=F32) for p in parts])

    def row_output():
        row = lax.broadcasted_iota(jnp.int32, st["gt"].shape, 0)
        gct_ref[0] = jnp.where(row >= DN_HEADS, st["suf_t"], st["pre_t"])

    return [col_project, col_sums, col_output], [row_project, row_sums, row_output]


_HALO = 8
_QKV_BLOCK = 256


def _qkv_conv(h_ext, wqkv_ref, cw_ref, qkv_ref, tm, between=()):
    rows = tm + 2 * _HALO
    between = list(between)
    nb = 3 * DN_WIDTH // _QKV_BLOCK
    per_block = -(-len(between) // nb)
    for j in range(nb):
        for _ in range(per_block):
            if between:
                between.pop(0)()
        cols = slice(j * _QKV_BLOCK, (j + 1) * _QKV_BLOCK)
        p = jnp.dot(h_ext, wqkv_ref[:, cols], preferred_element_type=F32)
        w = cw_ref[:, cols]
        y = (pltpu.roll(p, 1, axis=0)[_HALO:_HALO + tm] * w[0:1, :] + p[_HALO:_HALO + tm] * w[1:2, :]
             + pltpu.roll(p, rows - 1, axis=0)[_HALO:_HALO + tm] * w[2:3, :])
        y = _silu(y)
        if j * _QKV_BLOCK < 2 * DN_WIDTH:
            gain = DN_HEAD_DIM ** -0.5 if j * _QKV_BLOCK < DN_WIDTH else 1.0
            heads = [y[:, i * DN_HEAD_DIM:(i + 1) * DN_HEAD_DIM] for i in range(_QKV_BLOCK // DN_HEAD_DIM)]
            y = jnp.concatenate([yh * (lax.rsqrt(jnp.sum(yh * yh, axis=-1, keepdims=True) + EPS) * gain)
                                 for yh in heads], axis=-1)
        qkv_ref[0, :, cols] = y.astype(BF16)
    for job in between:
        job()


def _inproj_kernel(*refs, tm, latent):
    if latent:
        (x_ref, xp_ref, xn_ref, mod_ref, er_ref, ec_ref, wf_ref, wqkv_ref, wg_ref, wr_ref, wab_ref, wad_ref,
         wdt_ref, cw_ref, lvec_ref, cvec_ref, lblk_ref, ublk_ref,
         fv_ref, fg_ref, qkv_ref, sg_ref, rf_ref, rd_ref, aux_ref, gct_ref) = refs
    else:
        (x_ref, xp_ref, xn_ref, mod_ref, wqkv_ref, wab_ref, wad_ref, wdt_ref, cw_ref,
         lvec_ref, cvec_ref, lblk_ref, ublk_ref,
         qkv_ref, aux_ref, gct_ref) = refs
    t = pl.program_id(1)
    nt = pl.num_programs(1)
    x = x_ref[0]
    xp = xp_ref[0]
    xn = xn_ref[0]
    if latent:
        x = _add_pos(x, er_ref, ec_ref, t, tm)
        rows = tm // GRID_W
        last = er_ref.shape[0] - 1
        erp = er_ref[pl.ds(jnp.maximum(t * rows - 1, 0), 1), :]
        ern = er_ref[pl.ds(jnp.minimum((t + 1) * rows, last), 1), :]
        xp = xp + jnp.concatenate([jnp.broadcast_to(erp, (_HALO, D_MODEL // 2)),
                                   ec_ref[GRID_W - _HALO:GRID_W, :]], axis=-1)
        xn = xn + jnp.concatenate([jnp.broadcast_to(ern, (_HALO, D_MODEL // 2)), ec_ref[0:_HALO, :]], axis=-1)
    mod = mod_ref[0]
    shift = mod[:, 0:D_MODEL]
    scale = mod[:, D_MODEL:2 * D_MODEL]
    hf = _layer_norm(x) * (1.0 + scale) + shift
    h = hf.astype(BF16)
    hp = (_layer_norm(xp) * (1.0 + scale) + shift) * jnp.where(t > 0, 1.0, 0.0)
    hn = (_layer_norm(xn) * (1.0 + scale) + shift) * jnp.where(t < nt - 1, 1.0, 0.0)
    col_chain, row_chain = _decay_jobs(h, wab_ref, wad_ref, wdt_ref, lvec_ref, cvec_ref, lblk_ref, ublk_ref,
                                       aux_ref, gct_ref, tm)
    h_ext = jnp.concatenate([hp, hf, hn], axis=0).astype(BF16)
    if latent:
        _qkv_conv(h_ext, wqkv_ref, cw_ref, qkv_ref, tm)
        pf = jnp.dot(h, wf_ref[...], preferred_element_type=F32)
        fv_ref[0] = pf[:, :F_WIDTH].astype(BF16)
        fg_ref[0] = _silu(pf[:, F_WIDTH:]).astype(BF16)
        sg_ref[0] = _silu(jnp.dot(h, wg_ref[...], preferred_element_type=F32)).astype(BF16)
        pr = jnp.dot(h, wr_ref[...], preferred_element_type=F32)
        rf_ref[0] = jax.nn.sigmoid(pr[:, :D_MODEL]).astype(BF16)
        rd_ref[0] = jax.nn.sigmoid(pr[:, D_MODEL:]).astype(BF16)
        for job in col_chain + row_chain:
            job()
    else:
        jobs = [j for stage in zip(col_chain, row_chain) for j in stage + (lambda: None, lambda: None)]
        _qkv_conv(h_ext, wqkv_ref, cw_ref, qkv_ref, tm, jobs)


def _const_spec(shape):
    nd = len(shape)
    return pl.BlockSpec(shape, lambda b, t: (0,) * nd)


def _inproj(xin, mod3, mod_row_fn, tables, weights, consts, *, tm, latent):
    bsz, n, _ = xin.shape
    nt = n // tm
    wf, wqkv, wg, wr, wab, wad, wdt, cw = weights
    lvec, cvec, lblk, ublk = consts
    tok = lambda w: pl.BlockSpec((1, tm, w), lambda b, t: (b, t, 0))
    x_spec = pl.BlockSpec((1, tm, D_MODEL), lambda b, t: (b, t, 0))
    per = tm // _HALO
    xp_spec = pl.BlockSpec((1, _HALO, D_MODEL), lambda b, t: (b, jnp.maximum(t * per - 1, 0), 0))
    xn_spec = pl.BlockSpec((1, _HALO, D_MODEL), lambda b, t: (b, jnp.minimum((t + 1) * per, n // _HALO - 1), 0))
    mod_spec = pl.BlockSpec((1, 1, 3 * D_MODEL), lambda b, t: (mod_row_fn(b), 0, 0))
    gct_spec = pl.BlockSpec((1, AUX_GROUP, tm), lambda b, t: (b, 0, t))
    sds = lambda w, dt: jax.ShapeDtypeStruct((bsz, n, w), dt)
    gct_sds = jax.ShapeDtypeStruct((bsz, AUX_GROUP, n), F32)
    tail_in = [lvec, cvec, lblk, ublk]
    tail_specs = [_const_spec(a.shape) for a in tail_in]
    if latent:
        er, ec = tables
        ins = [xin, xin, xin, mod3, er, ec, wf, wqkv, wg, wr, wab, wad, wdt, cw] + tail_in
        in_specs = ([x_spec, xp_spec, xn_spec, mod_spec] + [_const_spec(a.shape) for a in ins[4:14]] + tail_specs)
        out_specs = [tok(F_WIDTH), tok(F_WIDTH), tok(3 * DN_WIDTH), tok(DN_WIDTH), tok(D_MODEL),
                     tok(D_MODEL), tok(LANES), gct_spec]
        out_shape = [sds(F_WIDTH, BF16), sds(F_WIDTH, BF16), sds(3 * DN_WIDTH, BF16), sds(DN_WIDTH, BF16),
                     sds(D_MODEL, BF16), sds(D_MODEL, BF16), sds(LANES, F32), gct_sds]
    else:
        ins = [xin, xin, xin, mod3, wqkv, wab, wad, wdt, cw] + tail_in
        in_specs = ([x_spec, xp_spec, xn_spec, mod_spec] + [_const_spec(a.shape) for a in ins[4:9]] + tail_specs)
        out_specs = [tok(3 * DN_WIDTH), tok(LANES), gct_spec]
        out_shape = [sds(3 * DN_WIDTH, BF16), sds(LANES, F32), gct_sds]
    return pl.pallas_call(
        functools.partial(_inproj_kernel, tm=tm, latent=latent),
        grid=(bsz, nt),
        in_specs=in_specs,
        out_specs=out_specs,
        out_shape=out_shape,
        compiler_params=pltpu.CompilerParams(dimension_semantics=("arbitrary", "arbitrary"),
                                             vmem_limit_bytes=VMEM_LIMIT),
        name="inproj_latent" if latent else "inproj_ctx",
    )(*ins)


def _block_diag2(m):
    lane = lax.broadcasted_iota(jnp.int32, m.shape, 1)
    z = jnp.zeros_like(m)
    return jnp.concatenate([jnp.where(lane < CHUNK, m, z), jnp.where(lane >= CHUNK, m, z)], axis=0).astype(BF16)


def _dn_masks():
    ri = lax.broadcasted_iota(jnp.int32, (CHUNK, 2 * CHUNK), 0)
    lane = lax.broadcasted_iota(jnp.int32, (CHUNK, 2 * CHUNK), 1)
    fwd = lane < CHUNK
    ci = jnp.where(fwd, lane, lane - CHUNK)
    bwd = jnp.logical_not(fwd)
    incl = (fwd & (ri >= ci)) | (bwd & (ri <= ci))
    strict = (fwd & (ri > ci)) | (bwd & (ri < ci))
    same = lambda k: (ri >> k) == (ci >> k)
    levels = [same(k + 1) & jnp.logical_not(same(k)) for k in range(1, int(math.log2(CHUNK)))]
    eye = jnp.where(ri == ci, 1.0, 0.0)
    return fwd, incl, strict, same(1), levels, eye


def _dn_side(refs, c, h):
    q_ref, k_ref, v_ref, aux_ref, lanes = refs
    rows = pl.ds(pl.multiple_of(c * CHUNK, CHUNK), CHUNK)
    aux = pltpu.roll(aux_ref[0, rows, :], (LANES - h) % LANES, axis=1)
    return (q_ref[0, rows, lanes].astype(F32), k_ref[0, rows, lanes].astype(F32),
            v_ref[0, rows, lanes].astype(F32), aux)


def _dn_prepare(pairs, h, scr, between=()):
    mq_s, b_s, egt_s = scr
    fwd, incl, strict, same1, levels, eye = _dn_masks()
    between = list(between)
    n_slots = 2 * len(levels) + 2
    per_slot = -(-len(between) // n_slots)

    def slot():
        for _ in range(per_slot):
            if between:
                between.pop(0)()

    col = lambda aux, j, d: aux[:, j * AUX_GROUP + d * DN_HEADS:j * AUX_GROUP + d * DN_HEADS + 1]
    zk = jnp.zeros((CHUNK, DN_HEAD_DIM), F32)
    sides, ls, decays, qks = [], [], [], []
    for (refs_f, cf, refs_b, cb, gt_ref, _, _) in pairs:
        qa, ka, va, auxa = _dn_side(refs_f, cf, h)
        qb, kb, vb, auxb = _dn_side(refs_b, cb, h)
        gram = _dot_nt(jnp.concatenate([jnp.concatenate([ka, kb], axis=1), jnp.concatenate([qa, qb], axis=1)], axis=0),
                       jnp.concatenate([jnp.concatenate([ka, zk], axis=1), jnp.concatenate([zk, kb], axis=1)], axis=0))
        kk, qk = gram[:CHUNK], gram[CHUNK:]
        both = lambda j: jnp.where(fwd, col(auxa, j, 0), col(auxb, j, 1))
        gr = jnp.concatenate([gt_ref[0, h, pl.ds(cf, 1), :], gt_ref[0, DN_HEADS + h, pl.ds(cb, 1), :]],
                             axis=1)
        decay = jnp.exp(jnp.where(incl, both(1) - gr, -jnp.inf))
        ls.append(jnp.where(strict, kk * decay * both(0), 0.0))
        sides.append((qa, ka, va, auxa, qb, kb, vb, auxb)); decays.append(decay); qks.append(qk)
    slot()

    xs = [eye - jnp.where(same1, l, 0.0) for l in ls]
    for lvl in levels:
        ts = [jnp.dot(x.astype(BF16), _block_diag2(jnp.where(lvl, l, 0.0)), preferred_element_type=F32)
              for x, l in zip(xs, ls)]
        slot()
        xs = [x - jnp.dot(t.astype(BF16), _block_diag2(x), preferred_element_type=F32) for x, t in zip(xs, ts)]
        slot()

    def split_rows(m):
        lane = lax.broadcasted_iota(jnp.int32, m.shape, 1)
        z = jnp.zeros_like(m)
        return jnp.concatenate([jnp.where(lane < CHUNK, m, z), jnp.where(lane >= CHUNK, m, z)], axis=0).astype(BF16)

    sols = []
    for x, (qa, ka, va, auxa, qb, kb, vb, auxb) in zip(xs, sides):
        rhs = jnp.concatenate([jnp.concatenate([va * col(auxa, 0, 0), ka * col(auxa, 3, 0)], axis=-1),
                               jnp.concatenate([vb * col(auxb, 0, 1), kb * col(auxb, 3, 1)], axis=-1)], axis=0)
        sols.append(jnp.dot(split_rows(x), rhs.astype(BF16), preferred_element_type=F32).astype(BF16))
    slot()
    mbs = []
    for sol, (qa, ka, va, auxa, qb, kb, vb, auxb), qk, decay in zip(sols, sides, qks, decays):
        kdt = jnp.concatenate([ka * col(auxa, 4, 0), kb * col(auxb, 4, 1)], axis=0).T
        lhs = jnp.concatenate([kdt, qk * decay], axis=0)
        mbs.append(jnp.dot(split_rows(lhs), sol, preferred_element_type=F32))
    while between:
        between.pop(0)()
    w = DN_HEAD_DIM
    for (refs_f, cf, refs_b, cb, gt_ref, pos, o0), mb, side in zip(pairs, mbs, sides):
        for d, (q, aux, c) in enumerate(((side[0], side[3], cf), (side[4], side[7], cb))):
            r = d * (w + CHUNK)
            b_s[d, pos] = mb[r:r + w, :w]
            mq_s[d, pos, 0:w, :] = mb[r:r + w, w:].astype(BF16)
            egt_s[d, pos] = jnp.broadcast_to(col(aux, 5, d)[0:1, :], (8, LANES))
            if o0 is not None:
                mq_s[d, pos, w:w + CHUNK, :] = (q * col(aux, 2, d) - mb[r + w:r + w + CHUNK, w:]).astype(BF16)
                o0[d][pl.ds(pl.multiple_of(c * CHUNK, CHUNK), CHUNK), :] = mb[r + w:r + w + CHUNK, :w]


def _dn_step(d, pos, scr, s_ref, o_ref=None, chunk=None):
    mq_s, b_s, egt_s = scr
    w = DN_HEAD_DIM
    s = s_ref[d]
    sb = s.astype(BF16)
    egt = egt_s[d, pos]
    s3 = (s.reshape(w // 8, 8, LANES) * egt[None]).reshape(w, LANES)
    if o_ref is None:
        ms = jnp.dot(mq_s[d, pos, 0:w, :], sb, preferred_element_type=F32)
    else:
        mqs = jnp.dot(mq_s[d, pos], sb, preferred_element_type=F32)
        ms = mqs[:w]
        rows = pl.ds(pl.multiple_of(chunk * CHUNK, CHUNK), CHUNK)
        o_ref[rows, :] = o_ref[rows, :] + mqs[w:]
    s_ref[d] = s3 - ms + b_s[d, pos]


_DN_GROUP = 8
_DN_FINISH_ROWS = 256
_DN_HEADS_PER_STEP = 2


def _dn_kernel(qx, kx, vx, auxx, gtx, qc, kc, vc, auxc, gtc, out_ref,
               mq_s, b_s, egt_s, of_s, ob_s, s_ref, *, ncx, ncc):
    scr = (mq_s, b_s, egt_s)
    ngroups = ncx // _DN_GROUP
    fr = min(_DN_FINISH_ROWS, ncx * CHUNK)

    def head(i, pending):
        h = pl.program_id(1) * _DN_HEADS_PER_STEP + i
        lanes = slice(i * DN_HEAD_DIM, (i + 1) * DN_HEAD_DIM)
        lat = (qx, kx, vx, auxx, lanes)
        ctx = (qc, kc, vc, auxc, lanes)

        def ctx_pair(p):
            return (ctx, p, ctx, ncc - 1 - p, gtc, p, None)

        def lat_pair(j):
            return (lat, j, lat, ncx - 1 - j, gtx, ncc + j, (of_s, ob_s))

        def ctx_step(p):
            _dn_step(0, p, scr, s_ref)
            _dn_step(1, p, scr, s_ref)

        def lat_step(j):
            _dn_step(0, ncc + j, scr, s_ref, of_s, j)
            _dn_step(1, ncc + j, scr, s_ref, ob_s, ncx - 1 - j)

        def lat_group(g):
            return [lat_pair(g * _DN_GROUP + j) for j in range(_DN_GROUP)]

        def lat_steps(g):
            return [functools.partial(lat_step, g * _DN_GROUP + j) for j in range(_DN_GROUP)]

        def emit_rows(r):
            rows = pl.ds(r * fr, fr)
            out_ref[0, rows, lanes] = (of_s[rows, :] + ob_s[rows, :]).astype(BF16)

        _dn_prepare([ctx_pair(p) for p in range(ncc)] + lat_group(0), h, scr, pending)
        s_ref[...] = jnp.zeros_like(s_ref)
        if ngroups > 1:
            _dn_prepare(lat_group(1), h, scr, [functools.partial(ctx_step, p) for p in range(ncc)] + lat_steps(0))

            def body(g, carry):
                _dn_prepare(lat_group(g), h, scr, lat_steps(g - 1))
                return carry

            lax.fori_loop(2, ngroups, body, 0)
        else:
            for p in range(ncc):
                ctx_step(p)
        return lat_steps(ngroups - 1) + [functools.partial(emit_rows, r) for r in range(ncx * CHUNK // fr)]

    pending = []
    for i in range(_DN_HEADS_PER_STEP):
        pending = head(i, pending)
    for job in pending:
        job()


def _deltanet(qkv_x, aux_x, gct_x, qkv_c, aux_c, gct_c):
    bsz, n, _ = qkv_x.shape
    nctx = qkv_c.shape[1]
    ncx, ncc = n // CHUNK, nctx // CHUNK
    nct = ncx + ncc
    assert ncx % _DN_GROUP == 0

    gct_x = gct_x.reshape(bsz, AUX_GROUP, ncx, CHUNK)
    gct_c = gct_c.reshape(bsz, AUX_GROUP, ncc, CHUNK)
    hps = _DN_HEADS_PER_STEP
    col = lambda rows, off: pl.BlockSpec((1, rows, hps * DN_HEAD_DIM), lambda b, h: (b, 0, off // hps + h))
    whole = lambda a: pl.BlockSpec((1,) + a.shape[1:], lambda b, h: (b,) + (0,) * (a.ndim - 1))
    in_specs = [col(n, 0), col(n, DN_HEADS), col(n, 2 * DN_HEADS), whole(aux_x), whole(gct_x),
                col(nctx, 0), col(nctx, DN_HEADS), col(nctx, 2 * DN_HEADS), whole(aux_c), whole(gct_c)]
    scratch = [pltpu.VMEM((N_DIR, nct, DN_HEAD_DIM + CHUNK, DN_HEAD_DIM), BF16),
               pltpu.VMEM((N_DIR, nct, DN_HEAD_DIM, DN_HEAD_DIM), F32),
               pltpu.VMEM((N_DIR, nct, 8, LANES), F32),
               pltpu.VMEM((n, DN_HEAD_DIM), F32),
               pltpu.VMEM((n, DN_HEAD_DIM), F32),
               pltpu.VMEM((N_DIR, DN_HEAD_DIM, DN_HEAD_DIM), F32)]
    return pl.pallas_call(
        functools.partial(_dn_kernel, ncx=ncx, ncc=ncc),
        grid=(bsz, DN_HEADS // hps),
        in_specs=in_specs,
        out_specs=col(n, 0),
        out_shape=jax.ShapeDtypeStruct((bsz, n, DN_WIDTH), BF16),
        scratch_shapes=scratch,
        compiler_params=pltpu.CompilerParams(dimension_semantics=("arbitrary", "arbitrary"),
                                             vmem_limit_bytes=VMEM_LIMIT),
        name="deltanet",
    )(qkv_x, qkv_x, qkv_x, aux_x, gct_x, qkv_c, qkv_c, qkv_c, aux_c, gct_c)


_F_ROWS = 1024


def _fourier_kernel(fv_ref, m_ref, r_ref, out_ref, pq_s, *, n4):
    k1 = pl.program_id(1)
    rb = min(_F_ROWS, n4)

    def combo(x, k):
        if k == 0:
            return x[0] + x[1] + x[2] + x[3], None
        if k == 2:
            return x[0] - x[1] + x[2] - x[3], None
        if k == 1:
            return x[0] - x[2], x[3] - x[1]
        return x[0] - x[2], x[1] - x[3]

    r = r_ref[...]
    for k in range(4):
        @pl.when(k1 == k)
        def _():
            def body(i, carry):
                r0 = pl.multiple_of(i * rb, rb)
                x = [fv_ref[0, pl.ds(j * n4 + r0, rb), :].astype(F32) for j in range(4)]
                d, e = combo(x, k)
                for g in range(F_GROUPS):
                    sl = slice(g * F_GROUP_DIM, (g + 1) * F_GROUP_DIM)
                    if e is None:
                        pq = jnp.dot(d[:, sl].astype(BF16), r[:F_GROUP_DIM, :], preferred_element_type=F32)
                    else:
                        de = jnp.concatenate([d[:, sl], e[:, sl]], axis=-1).astype(BF16)
                        pq = jnp.dot(de, r, preferred_element_type=F32)
                    pq_s[pl.ds(r0, rb), sl] = pq[:, :F_GROUP_DIM].astype(BF16)
                    pq_s[pl.ds(n4 + r0, rb), sl] = pq[:, F_GROUP_DIM:].astype(BF16)
                return carry
            lax.fori_loop(0, n4 // rb, body, 0)

    def rows(i, carry):
        r0 = pl.multiple_of(i * rb, rb)
        res = jnp.dot(m_ref[0, pl.ds(r0, rb), :], pq_s[...], preferred_element_type=F32)
        for g in range(F_GROUPS):
            out_ref[0, g, pl.ds(k1 + 4 * r0, rb, stride=4), :] = res[:, g * F_GROUP_DIM:(g + 1) * F_GROUP_DIM]
        return carry
    lax.fori_loop(0, n4 // rb, rows, 0)


def _fourier(fv, mcat, rmat):
    bsz, n, _ = fv.shape
    n4 = n // 4
    return pl.pallas_call(
        functools.partial(_fourier_kernel, n4=n4),
        grid=(bsz, 4),
        in_specs=[pl.BlockSpec((1, n, F_WIDTH), lambda b, k: (b, 0, 0)),
                  pl.BlockSpec((1, n4, 2 * n4), lambda b, k: (k, 0, 0)),
                  pl.BlockSpec((2 * F_GROUP_DIM, 2 * F_GROUP_DIM), lambda b, k: (0, 0))],
        out_specs=pl.BlockSpec((1, F_GROUPS, n, F_GROUP_DIM), lambda b, k: (b, 0, 0, 0)),
        out_shape=jax.ShapeDtypeStruct((bsz, F_GROUPS, n, F_GROUP_DIM), F32),
        scratch_shapes=[pltpu.VMEM((2 * n4, F_WIDTH), BF16)],
        compiler_params=pltpu.CompilerParams(dimension_semantics=("arbitrary", "arbitrary"),
                                             vmem_limit_bytes=VMEM_LIMIT),
        name="fourier",
    )(fv, mcat, rmat)


def _merge_kernel(x_ref, mod_ref, er_ref, ec_ref, mix_ref, fg_ref, od_ref, sg_ref, rf_ref, rd_ref,
                  wfm_ref, wfo_ref, wdo_ref, wo_ref, nw_ref, lng_ref, lnb_ref, out_ref, *, tm, alpha):
    t = pl.program_id(1)
    mixed = jnp.concatenate(
        [jnp.dot(mix_ref[0, g].astype(BF16), wfm_ref[g], preferred_element_type=F32)
         for g in range(F_GROUPS)], axis=-1)
    y_f = jnp.dot((mixed * fg_ref[0].astype(F32)).astype(BF16), wfo_ref[...], preferred_element_type=F32)
    od = od_ref[0].astype(F32)
    nw = nw_ref[...]
    heads = []
    for i in range(DN_HEADS):
        oh = od[:, i * DN_HEAD_DIM:(i + 1) * DN_HEAD_DIM]
        heads.append(oh * lax.rsqrt(jnp.mean(oh * oh, axis=-1, keepdims=True) + EPS) * nw)
    og = (jnp.concatenate(heads, axis=-1) * sg_ref[0].astype(F32)).astype(BF16)
    y_d = jnp.dot(og, wdo_ref[...], preferred_element_type=F32)
    m = rf_ref[0].astype(F32) * y_f + rd_ref[0].astype(F32) * y_d
    o = jnp.dot(m.astype(BF16), wo_ref[...], preferred_element_type=F32)
    gate = mod_ref[0][:, 2 * D_MODEL:3 * D_MODEL]
    x = _add_pos(x_ref[0], er_ref, ec_ref, t, tm)
    out_ref[0] = _layer_norm(alpha * x + gate * o) * lng_ref[...] + lnb_ref[...]


def _merge(x, mod3, er, ec, mixed, fg, od, sg, rf, rd, wfm, wfo, wdo, wo, norm_w, ln_g, ln_b, *, tm, alpha):
    bsz, n, _ = x.shape
    tok = lambda w: pl.BlockSpec((1, tm, w), lambda b, t: (b, t, 0))
    consts = [wfm, wfo, wdo, wo, norm_w, ln_g, ln_b]
    in_specs = [tok(D_MODEL),
                pl.BlockSpec((1, 1, 3 * D_MODEL), lambda b, t: (b, 0, 0)),
                _const_spec(er.shape), _const_spec(ec.shape),
                pl.BlockSpec((1, F_GROUPS, tm, F_GROUP_DIM), lambda b, t: (b, 0, t, 0)),
                tok(F_WIDTH), tok(DN_WIDTH), tok(DN_WIDTH), tok(D_MODEL), tok(D_MODEL)]
    in_specs += [_const_spec(a.shape) for a in consts]
    return pl.pallas_call(
        functools.partial(_merge_kernel, tm=tm, alpha=alpha),
        grid=(bsz, n // tm),
        in_specs=in_specs,
        out_specs=tok(D_MODEL),
        out_shape=jax.ShapeDtypeStruct((bsz, n, D_MODEL), F32),
        compiler_params=pltpu.CompilerParams(dimension_semantics=("arbitrary", "arbitrary"),
                                             vmem_limit_bytes=VMEM_LIMIT),
        name="merge",
    )(x, mod3, er, ec, mixed, fg, od, sg, rf, rd, *consts)


def _pos_tables(rows):
    quarter = D_MODEL // 4
    omega = 1.0 / (10000.0 ** (np.arange(quarter, dtype=np.float64) / quarter))
    pr = np.arange(rows, dtype=np.float64)[:, None] * omega
    pc = np.arange(GRID_W, dtype=np.float64)[:, None] * omega
    er = np.concatenate([np.sin(pr), np.cos(pr)], axis=-1)
    ec = np.concatenate([np.sin(pc), np.cos(pc)], axis=-1)
    return jnp.asarray(er, F32), jnp.asarray(ec, F32)


def _chunk_sum_matrices(tm):
    i = np.arange(tm)
    same = (i[:, None] // CHUNK) == (i[None, :] // CHUNK)
    lower = same & (i[:, None] >= i[None, :])
    upper = same & (i[:, None] <= i[None, :])
    return jnp.asarray(lower, BF16), jnp.asarray(upper, BF16)


def _channel_dft_matrix():
    c = np.arange(F_GROUP_DIM)
    ang = 2.0 * np.pi * ((c[:, None] * c[None, :]) % F_GROUP_DIM) / F_GROUP_DIM
    cc, sc = np.cos(ang), np.sin(ang)
    r = np.block([[cc, sc], [sc, -cc]]) / math.sqrt(F_GROUP_DIM)
    return jnp.asarray(r, F32).astype(BF16)


def _position_dft_matrices(n):
    n4 = n // 4
    idx = jnp.arange(n4, dtype=jnp.int32)
    beta = ((idx[:, None] * idx[None, :]) % n4).astype(F32) * (2.0 * math.pi / n4)
    alpha = (jnp.arange(4, dtype=jnp.int32)[:, None] * idx[None, :]).astype(F32) * (2.0 * math.pi / n)
    cb, sb = jnp.cos(beta)[None], jnp.sin(beta)[None]
    ca, sa = jnp.cos(alpha)[:, None, :], jnp.sin(alpha)[:, None, :]
    scale = 1.0 / math.sqrt(n)
    return jnp.concatenate([(ca * cb - sa * sb) * scale, (sa * cb + ca * sb) * -scale], axis=-1).astype(BF16)


def _aux_weights(w_beta, w_decay, a_log, dt_bias):
    reps = LANES // AUX_GROUP
    wab = jnp.tile(w_beta, (1, reps)).astype(BF16)
    wad = jnp.tile(w_decay, (1, reps)).astype(BF16)
    wdt = w_decay.T.astype(BF16)
    al = a_log.reshape(1, AUX_GROUP).astype(F32)
    db = dt_bias.reshape(1, AUX_GROUP).astype(F32)
    lvec = jnp.concatenate([jnp.tile(al, (1, reps)), jnp.tile(db, (1, reps)),
                            jnp.zeros((6, LANES), F32)], axis=0)
    cvec = jnp.concatenate([al.T, db.T], axis=1)
    return wab, wad, wdt, lvec, cvec


def kernel(x, c, ctx, c_ctx, w_mod, b_mod, w_in, conv_w, a_log, dt_bias, dn_norm_w, w_dn_out, w_fmix,
           w_f_out, w_out, ln_g, ln_b):
    depth = w_mod.shape[0]
    assert depth == 1, "single-layer configuration"
    bsz, n, _ = x.shape
    nctx = ctx.shape[1]
    assert n % (4 * CHUNK) == 0 and nctx % CHUNK == 0 and bsz <= 8
    alpha = (2 * depth) ** 0.25
    tm = min(512, n)
    tmc = min(256, nctx)

    cc = jnp.zeros((16, D_MODEL), F32).at[:bsz].set(c).at[8].set(c_ctx)
    mod3 = _modulation(cc, w_mod[0], b_mod[0].reshape(1, -1)).reshape(16, 1, 3 * D_MODEL)

    w = w_in[0]
    o0, o1, o2, o3, o4, o5, o6 = (int(v) for v in np.cumsum(
        (F_WIDTH, F_WIDTH, 3 * DN_WIDTH, DN_WIDTH, AUX_GROUP, AUX_GROUP, D_MODEL)))
    wf = w[:, :o1].astype(BF16)
    wqkv = w[:, o1:o2].astype(BF16)
    wg = w[:, o2:o3].astype(BF16)
    wr = w[:, o5:].astype(BF16)
    wab, wad, wdt, lvec, cvec = _aux_weights(w[:, o3:o4], w[:, o4:o5], a_log[0], dt_bias[0])
    er, ec = _pos_tables(n // GRID_W)

    lblk, ublk = _chunk_sum_matrices(tm)
    fv, fg, qkv_x, sg, rf, rd, aux_x, gct_x = _inproj(
        x, mod3, lambda b: b, (er, ec), (wf, wqkv, wg, wr, wab, wad, wdt, conv_w[0]), (lvec, cvec, lblk, ublk),
        tm=tm, latent=True)
    lblk_c, ublk_c = _chunk_sum_matrices(tmc)
    qkv_c, aux_c, gct_c = _inproj(
        ctx, mod3, lambda b: 8, None, (wf, wqkv, wg, wr, wab, wad, wdt, conv_w[0]), (lvec, cvec, lblk_c, ublk_c),
        tm=tmc, latent=False)

    od = _deltanet(qkv_x, aux_x, gct_x, qkv_c, aux_c, gct_c)
    mixed = _fourier(fv, _position_dft_matrices(n), _channel_dft_matrix())
    return _merge(x, mod3, er, ec, mixed, fg, od, sg, rf, rd,
                  w_fmix[0].astype(BF16), w_f_out[0].astype(BF16), w_dn_out[0].astype(BF16),
                  w_out[0].astype(BF16), dn_norm_w[0].reshape(1, -1).astype(F32),
                  ln_g[0].reshape(1, -1), ln_b[0].reshape(1, -1), tm=tm, alpha=alpha)
```
